```python
import jax, jax.numpy as jnp
from jax import lax
import numpy as np

D_MODEL = 1024
BATCH = 16
SEQ = 2048
DEPTH = 1

CTX_LEN = 256
GRID_W = 64
CONV_W = 512
MLSTM_HEADS = 4
HEAD_DIM = 128
MLSTM_W = MLSTM_HEADS * HEAD_DIM
MIX_W = CONV_W + MLSTM_W
CONV_COLS = 3 * CONV_W
N_GATE_COLS = 4 * MLSTM_HEADS
IN_COLS = CONV_COLS + 4 * MLSTM_W + N_GATE_COLS
FFN_DIM = 2816
CHUNK = 128
N_MOD = 9
EPS = 1e-6

kernel_name = "hybrid_conv_mlstm_macaron_dit_layer"


def _rmsnorm(x, g):
    xf = x.astype(jnp.float32)
    y = xf * lax.rsqrt(jnp.mean(xf * xf, axis=-1, keepdims=True) + EPS)
    return y.astype(x.dtype) * g


def _modulation(cvec, w_mod, b_mod):
    m = (jax.nn.silu(cvec) @ w_mod + b_mod).reshape(cvec.shape[0], N_MOD, D_MODEL)
    return tuple(m[:, i, None, :] for i in range(N_MOD))


def _swiglu(h, w_up, w_down):
    a, b = jnp.split(h @ w_up, 2, axis=-1)
    return (jax.nn.silu(a) * b) @ w_down


def _ffn_sublayer(x, w_up, w_down, g_pre, g_post, shift, scale, gate):
    h = _rmsnorm(x, g_pre) * (1 + scale) + shift
    return x + 0.5 * gate * _rmsnorm(_swiglu(h, w_up, w_down), g_post)


def _conv3_grid(u, w):
    bn, length, ch = u.shape
    rows = length // GRID_W
    gp = jnp.pad(u.reshape(bn, rows, GRID_W, ch), ((0, 0), (0, 0), (1, 1), (0, 0)))
    out = w[0] * gp[:, :, :-2] + w[1] * gp[:, :, 1:-1] + w[2] * gp[:, :, 2:]
    return out.reshape(bn, length, ch)


def _conv3_seq(u, w):
    up = jnp.pad(u, ((0, 0), (1, 1), (0, 0)))
    return w[0] * up[:, :-2] + w[1] * up[:, 1:-1] + w[2] * up[:, 2:]


def _short_conv(p, conv_w, conv_fn):
    bg, cg, u = jnp.split(p, 3, axis=-1)
    return bg * conv_fn(cg * u, conv_w)


def _zero_state(bn):
    return (jnp.zeros((bn, MLSTM_HEADS, HEAD_DIM, HEAD_DIM), jnp.float32),
            jnp.zeros((bn, MLSTM_HEADS, HEAD_DIM), jnp.float32),
            jnp.zeros((bn, MLSTM_HEADS), jnp.float32))


def _mlstm_chunkwise(q, k, v, log_i, log_f, state, with_outputs):
    bn, nh, length, dh = q.shape
    nc = length // CHUNK

    def to_chunks(a):
        return jnp.moveaxis(a.reshape(a.shape[:2] + (nc, CHUNK) + a.shape[3:]), 2, 0)

    causal = jnp.tril(jnp.ones((CHUNK, CHUNK), dtype=bool))

    def step(carry, xs):
        c_st, n_st, m_st = carry
        qc, kc, vc, ic, fc = xs
        b = jnp.cumsum(fc, axis=-1)
        b_last = b[..., -1]
        w_end = b_last[..., None] - b + ic
        m_new = jnp.maximum(b_last + m_st, jnp.max(w_end, axis=-1))
        decay = jnp.exp(b_last + m_st - m_new)
        w = jnp.exp(w_end - m_new[..., None])
        c_new = decay[..., None, None] * c_st + jnp.einsum('bhs,bhsd,bhse->bhde', w, vc, kc)
        n_new = decay[..., None] * n_st + jnp.einsum('bhs,bhse->bhe', w, kc)
        if with_outputs:
            dmat = b[..., :, None] - b[..., None, :] + ic[..., None, :]
            dmat = jnp.where(causal, dmat, -jnp.inf)
            m_inter = b + m_st[..., None]
            m_q = jnp.maximum(m_inter, jnp.max(dmat, axis=-1))
            s = jnp.einsum('bhje,bhse->bhjs', qc, kc) * jnp.exp(dmat - m_q[..., None])
            g = jnp.exp(m_inter - m_q)
            num = g[..., None] * jnp.einsum('bhde,bhje->bhjd', c_st, qc) + jnp.einsum('bhjs,bhsd->bhjd', s, vc)
            den = g * jnp.einsum('bhe,bhje->bhj', n_st, qc) + jnp.sum(s, axis=-1)
            h = num / jnp.maximum(jnp.abs(den), jnp.exp(-m_q))[..., None]
        else:
            h = None
        return (c_new, n_new, m_new), h

    state, hs = lax.scan(step, state, tuple(to_chunks(a) for a in (q, k, v, log_i, log_f)))
    if with_outputs:
        hs = jnp.moveaxis(hs, 0, 2).reshape(bn, nh, length, dh)
    return state, hs


def _mlstm_bidir(p, b_gates, state_f, state_b, with_outputs):
    bn, length, _ = p.shape
    heads = lambda a: a.reshape(bn, length, MLSTM_HEADS, HEAD_DIM).transpose(0, 2, 1, 3).astype(jnp.float32)
    q = heads(p[..., 0:MLSTM_W])
    k = heads(p[..., MLSTM_W:2 * MLSTM_W]) * (HEAD_DIM ** -0.5)
    v = heads(p[..., 2 * MLSTM_W:3 * MLSTM_W])
    o = p[..., 3 * MLSTM_W:4 * MLSTM_W]
    gates = (p[..., 4 * MLSTM_W:] + b_gates).astype(jnp.float32)
    gates = gates.reshape(bn, length, 4, MLSTM_HEADS).transpose(2, 0, 3, 1)
    i_f, i_b = gates[0], gates[1]
    lf_f, lf_b = jax.nn.log_sigmoid(gates[2]), jax.nn.log_sigmoid(gates[3])
    flip = lambda a: jnp.flip(a, axis=2)
    st_f, h_f = _mlstm_chunkwise(q, k, v, i_f, lf_f, state_f, with_outputs)
    st_b, h_b = _mlstm_chunkwise(flip(q), flip(k), flip(v), flip(i_b), flip(lf_b), state_b, with_outputs)
    h = h_f + flip(h_b) if with_outputs else None
    return h, o, st_f, st_b


def _mlstm_out(h, o, mh_norm):
    bn, nh, length, dh = h.shape
    h = h.transpose(0, 2, 1, 3)
    h = h * lax.rsqrt(jnp.mean(h * h, axis=-1, keepdims=True) + EPS)
    h = h.reshape(bn, length, nh * dh).astype(o.dtype) * mh_norm
    return jax.nn.sigmoid(o) * h


def _mixer(h_lat, h_ctx, w_in, b_gates, conv_w, mh_norm, w_out, last):
    bn = h_lat.shape[0]
    zero = _zero_state(bn)
    if last:
        _, _, st_f, st_b = _mlstm_bidir(h_ctx @ w_in[:, CONV_COLS:], b_gates, zero, zero, False)
        y_ctx = None
    else:
        pc = h_ctx @ w_in
        hc, oc, st_f, st_b = _mlstm_bidir(pc[..., CONV_COLS:], b_gates, zero, zero, True)
        conv_c = _short_conv(pc[..., :CONV_COLS], conv_w, _conv3_seq)
        y_ctx = jnp.concatenate([conv_c, _mlstm_out(hc, oc, mh_norm)], axis=-1) @ w_out
    pl = h_lat @ w_in
    hl, ol, _, _ = _mlstm_bidir(pl[..., CONV_COLS:], b_gates, st_f, st_b, True)
    conv_l = _short_conv(pl[..., :CONV_COLS], conv_w, _conv3_grid)
    y_lat = jnp.concatenate([conv_l, _mlstm_out(hl, ol, mh_norm)], axis=-1) @ w_out
    return y_lat, y_ctx


def setup_inputs(seed: int = 0) -> dict:
    key = jax.random.key(seed)
    ks = jax.random.split(key, 20)
    d = D_MODEL
    nrm = lambda k, shape, s: jax.random.normal(k, shape, jnp.float32) * s
    forget_bias = jnp.tile(jnp.linspace(3.0, 6.0, MLSTM_HEADS, dtype=jnp.float32), 2)
    b_gates = jnp.concatenate([nrm(ks[14], (DEPTH, 2 * MLSTM_HEADS), 0.1),
                               forget_bias[None] + nrm(ks[15], (DEPTH, 2 * MLSTM_HEADS), 0.1)], axis=-1)
    return {
        "x": nrm(ks[0], (BATCH, SEQ, d), 1.0),
        "c": nrm(ks[1], (BATCH, d), 1.0),
        "ctx": nrm(ks[2], (BATCH, CTX_LEN, d), 1.0),
        "c_ctx": nrm(ks[3], (d,), 1.0),
        "w_mod": nrm(ks[4], (DEPTH, d, N_MOD * d), 0.5 * d ** -0.5),
        "b_mod": nrm(ks[5], (DEPTH, N_MOD * d), 0.02),
        "norm_g": 1.0 + nrm(ks[6], (DEPTH, 6, d), 0.05),
        "ffn1_up": nrm(ks[7], (DEPTH, d, 2 * FFN_DIM), d ** -0.5),
        "ffn1_down": nrm(ks[8], (DEPTH, FFN_DIM, d), FFN_DIM ** -0.5),
        "ffn2_up": nrm(ks[9], (DEPTH, d, 2 * FFN_DIM), d ** -0.5),
        "ffn2_down": nrm(ks[10], (DEPTH, FFN_DIM, d), FFN_DIM ** -0.5),
        "w_in": nrm(ks[11], (DEPTH, d, IN_COLS), d ** -0.5),
        "b_gates": b_gates,
        "conv_w": nrm(ks[12], (DEPTH, 3, CONV_W), 0.5),
        "mh_norm": 1.0 + nrm(ks[13], (DEPTH, MLSTM_W), 0.05),
        "w_out": nrm(ks[16], (DEPTH, MIX_W, d), MIX_W ** -0.5),
    }


def reference(x, c, ctx, c_ctx, w_mod, b_mod, norm_g, ffn1_up, ffn1_down, ffn2_up, ffn2_down,
              w_in, b_gates, conv_w, mh_norm, w_out):
    for l in range(DEPTH):
        last = l == DEPTH - 1
        g = norm_g[l]
        m_lat = _modulation(c, w_mod[l], b_mod[l])
        m_ctx = _modulation(c_ctx[None], w_mod[l], b_mod[l])
        x = _ffn_sublayer(x, ffn1_up[l], ffn1_down[l], g[0], g[1], *m_lat[0:3])
        ctx = _ffn_sublayer(ctx, ffn1_up[l], ffn1_down[l], g[0], g[1], *m_ctx[0:3])
        h_lat = _rmsnorm(x, g[2]) * (1 + m_lat[4]) + m_lat[3]
        h_ctx = _rmsnorm(ctx, g[2]) * (1 + m_ctx[4]) + m_ctx[3]
        y_lat, y_ctx = _mixer(h_lat, h_ctx, w_in[l], b_gates[l], conv_w[l], mh_norm[l], w_out[l], last)
        x = x + m_lat[5] * _rmsnorm(y_lat, g[3])
        x = _ffn_sublayer(x, ffn2_up[l], ffn2_down[l], g[4], g[5], *m_lat[6:9])
        if not last:
            ctx = ctx + m_ctx[5] * _rmsnorm(y_ctx, g[3])
            ctx = _ffn_sublayer(ctx, ffn2_up[l], ffn2_down[l], g[4], g[5], *m_ctx[6:9])
    return x
```

```python
import functools

import jax
import jax.numpy as jnp
from jax import lax
from jax.experimental import pallas as pl
from jax.experimental.pallas import tpu as pltpu

F32 = jnp.float32
BF16 = jnp.bfloat16

GRID_W = 64
CONV_W = 512
N_HEADS = 4
HEAD_DIM = 128
MLSTM_W = N_HEADS * HEAD_DIM
CHUNK = 128
N_MOD = 9
EPS = 1e-6
N_GATES = 4 * N_HEADS
MOD_ROWS = 24
LANES = 128
MXU_COLS = 256
VMEM_LIMIT = 56 * 1024 * 1024

FFN_CHUNK = 512


def _dot(a, b):
    return jnp.dot(a, b, preferred_element_type=F32)


def _rms(x):
    return x * lax.rsqrt(jnp.mean(x * x, axis=-1, keepdims=True) + EPS)


def _resident(shape):
    nd = len(shape)
    return pl.BlockSpec(shape, lambda *_: (0,) * nd, pipeline_mode=pl.Buffered(1))


def _mod_kernel(c_ref, w_ref, b_ref, o_ref):
    cv = c_ref[...]
    s = (cv * jax.nn.sigmoid(cv)).astype(BF16)
    o_ref[...] = _dot(s, w_ref[...].astype(BF16)) + b_ref[...]


def _modulation(cvec, w_mod, b_mod):
    d = cvec.shape[1]
    n = w_mod.shape[1]
    bn = n // N_MOD
    return pl.pallas_call(
        _mod_kernel,
        grid=(N_MOD,),
        in_specs=[pl.BlockSpec((MOD_ROWS, d), lambda j: (0, 0)),
                  pl.BlockSpec((d, bn), lambda j: (0, j)),
                  pl.BlockSpec((1, bn), lambda j: (0, j))],
        out_specs=pl.BlockSpec((MOD_ROWS, bn), lambda j: (0, j)),
        out_shape=jax.ShapeDtypeStruct((MOD_ROWS, n), F32),
        compiler_params=pltpu.CompilerParams(dimension_semantics=("arbitrary",),
                                             vmem_limit_bytes=VMEM_LIMIT),
        name="modulation",
    )(cvec, w_mod, b_mod.reshape(1, n))


def _ffn_tile(x, g_pre, g_post, shift, scale, gate, wup_ref, wdn_ref):
    ffn = wdn_ref.shape[0]
    h = (_rms(x) * g_pre * (1.0 + scale) + shift).astype(BF16)
    acc = None
    for c0 in range(0, ffn, FFN_CHUNK):
        cw = min(FFN_CHUNK, ffn - c0)
        a = _dot(h, wup_ref[:, c0:c0 + cw])
        b = _dot(h, wup_ref[:, ffn + c0:ffn + c0 + cw])
        act = (a * jax.nn.sigmoid(a) * b).astype(BF16)
        part = _dot(act, wdn_ref[c0:c0 + cw, :])
        acc = part if acc is None else acc + part
    return x + 0.5 * gate * (_rms(acc) * g_post)


def _log_sigmoid(x):
    return jnp.minimum(x, 0.0) - jnp.log1p(jnp.exp(-jnp.abs(x)))


def _gates_t(h, wg_ref, bg_ref):
    g = lax.dot_general(wg_ref[...], h, (((1,), (1,)), ((), ())), preferred_element_type=F32)
    g = g + bg_ref[...]
    half = N_GATES // 2
    return jnp.concatenate([g[:half], _log_sigmoid(g[half:])], axis=0)


def _ffn_proj_kernel(x_ref, mod_ref, g_ref, wup_ref, wdn_ref, wc_ref, wq_ref, wg_ref, bg_ref, cw_ref,
                     x1_ref, conv_ref, qkvo_ref, gt_ref):
    x1 = _ffn_tile(x_ref[...], g_ref[0], g_ref[1], mod_ref[0], mod_ref[1], mod_ref[2], wup_ref, wdn_ref)
    x1_ref[...] = x1
    h = (_rms(x1) * g_ref[2] * (1.0 + mod_ref[4]) + mod_ref[3]).astype(BF16)
    tm = h.shape[0]

    bg = _dot(h, wc_ref[:, 0:CONV_W])
    cu = _dot(h, wc_ref[:, CONV_W:2 * CONV_W]) * _dot(h, wc_ref[:, 2 * CONV_W:3 * CONV_W])
    col = lax.broadcasted_iota(jnp.int32, (tm, 1), 0) % GRID_W
    prev = jnp.where(col == 0, 0.0, pltpu.roll(cu, 1, 0))
    nxt = jnp.where(col == GRID_W - 1, 0.0, pltpu.roll(cu, tm - 1, 0))
    conv = bg * (cw_ref[0] * prev + cw_ref[1] * cu + cw_ref[2] * nxt)
    conv_ref[...] = conv.astype(BF16)

    kscale = HEAD_DIM ** -0.5
    for j in range(4):
        seg = _dot(h, wq_ref[:, j * MLSTM_W:(j + 1) * MLSTM_W])
        if j == 1:
            seg = seg * kscale
        qkvo_ref[:, j * MLSTM_W:(j + 1) * MLSTM_W] = seg.astype(BF16)

    gt_ref[...] = _gates_t(h, wg_ref, bg_ref)


def _ffn_proj(x2d, mod4, g3, wup, wdn, wc, wq, wg, bg, cw3, seq, tm):
    n, d = x2d.shape
    tpb = seq // tm
    nb = n // seq
    tile = lambda w: pl.BlockSpec((tm, w), lambda i: (i, 0))
    return pl.pallas_call(
        _ffn_proj_kernel,
        grid=(n // tm,),
        in_specs=[tile(d),
                  pl.BlockSpec((None, N_MOD, 1, d), lambda i: (i // tpb, 0, 0, 0)),
                  _resident(g3.shape), _resident(wup.shape), _resident(wdn.shape),
                  _resident(wc.shape), _resident(wq.shape), _resident(wg.shape),
                  _resident(bg.shape), _resident(cw3.shape)],
        out_specs=[tile(d), tile(CONV_W), tile(4 * MLSTM_W),
                   pl.BlockSpec((None, N_GATES, tm), lambda i: (i // tpb, 0, i % tpb))],
        out_shape=[jax.ShapeDtypeStruct((n, d), F32),
                   jax.ShapeDtypeStruct((n, CONV_W), BF16),
                   jax.ShapeDtypeStruct((n, 4 * MLSTM_W), BF16),
                   jax.ShapeDtypeStruct((nb, N_GATES, seq), F32)],
        compiler_params=pltpu.CompilerParams(dimension_semantics=("arbitrary",),
                                             vmem_limit_bytes=VMEM_LIMIT),
        name="ffn1_inproj",
    )(x2d, mod4, g3, wup, wdn, wc, wq, wg, bg, cw3)


def _ffn_proj_ctx_kernel(x_ref, mod_ref, g_ref, wup_ref, wdn_ref, wkv_ref, wg_ref, bg_ref, kv_ref, gt_ref):
    x1 = _ffn_tile(x_ref[...], g_ref[0], g_ref[1], mod_ref[0], mod_ref[1], mod_ref[2], wup_ref, wdn_ref)
    h = (_rms(x1) * g_ref[2] * (1.0 + mod_ref[4]) + mod_ref[3]).astype(BF16)
    kscale = HEAD_DIM ** -0.5
    kv_ref[:, 0:MLSTM_W] = (_dot(h, wkv_ref[:, 0:MLSTM_W]) * kscale).astype(BF16)
    kv_ref[:, MLSTM_W:2 * MLSTM_W] = _dot(h, wkv_ref[:, MLSTM_W:2 * MLSTM_W]).astype(BF16)
    gt_ref[...] = _gates_t(h, wg_ref, bg_ref)


def _ffn_proj_ctx(c2d, mod4, g3, wup, wdn, wkv, wg, bg, ctx_len, ctx_row):
    n, d = c2d.shape
    tm = ctx_len
    return pl.pallas_call(
        _ffn_proj_ctx_kernel,
        grid=(n // tm,),
        in_specs=[pl.BlockSpec((tm, d), lambda i: (i, 0)),
                  pl.BlockSpec((None, N_MOD, 1, d), lambda i: (ctx_row, 0, 0, 0)),
                  _resident(g3.shape), _resident(wup.shape), _resident(wdn.shape),
                  _resident(wkv.shape), _resident(wg.shape), _resident(bg.shape)],
        out_specs=[pl.BlockSpec((tm, 2 * MLSTM_W), lambda i: (i, 0)),
                   pl.BlockSpec((None, N_GATES, tm), lambda i: (i, 0, 0))],
        out_shape=[jax.ShapeDtypeStruct((n, 2 * MLSTM_W), BF16),
                   jax.ShapeDtypeStruct((n // tm, N_GATES, tm), F32)],
        compiler_params=pltpu.CompilerParams(dimension_semantics=("arbitrary",),
                                             vmem_limit_bytes=VMEM_LIMIT),
        name="ffn1_inproj_ctx",
    )(c2d, mod4, g3, wup, wdn, wkv, wg, bg)


def _chunk_cumsum(x, reverse):
    pos = lax.broadcasted_iota(jnp.int32, x.shape, 1) % CHUNK
    n = x.shape[1]
    d = 1
    while d < CHUNK:
        if reverse:
            x = x + jnp.where(pos < CHUNK - d, pltpu.roll(x, n - d, 1), 0.0)
        else:
            x = x + jnp.where(pos >= d, pltpu.roll(x, d, 1), 0.0)
        d *= 2
    return x


def _chunk_rows(c):
    if isinstance(c, int):
        return pl.ds(c * CHUNK, CHUNK)
    return pl.ds(pl.multiple_of(c * CHUNK, CHUNK), CHUNK)


def _lane_pick(row, lane):
    idx = lax.broadcasted_iota(jnp.int32, row.shape, 1)
    return jnp.sum(jnp.where(idx == lane, row, 0.0), axis=1, keepdims=True)


def _mlstm_chunk(kc, vc, i_row, b_row, s_ref, m, reverse, qc):
    b_last = _lane_pick(b_row, 0 if reverse else CHUNK - 1)
    w_end = b_last - b_row + i_row
    m_new = jnp.maximum(b_last + m, jnp.max(w_end, axis=1, keepdims=True))
    decay = jnp.exp(b_last + m - m_new)
    w_row = jnp.exp(w_end - m_new)
    k_t = kc.astype(F32).T
    unit = (lax.broadcasted_iota(jnp.int32, (CHUNK, HEAD_DIM), 1) == 0).astype(BF16)
    v_aug = jnp.concatenate([vc, unit], axis=1)
    s_old = s_ref[...]
    s_ref[...] = decay * s_old + _dot((k_t * w_row).astype(BF16), v_aug)
    if qc is None:
        return m_new, None
    row = lax.broadcasted_iota(jnp.int32, (CHUNK, CHUNK), 0)
    col = lax.broadcasted_iota(jnp.int32, (CHUNK, CHUNK), 1)
    b_col = jnp.sum(jnp.where(row == col, b_row, 0.0), axis=1, keepdims=True)
    mask = (col >= row) if reverse else (col <= row)
    dmat = jnp.where(mask, b_col - b_row + i_row, -jnp.inf)
    m_q = jnp.maximum(b_col + m, jnp.max(dmat, axis=1, keepdims=True))
    s = (_dot(qc, k_t.astype(BF16)) * jnp.exp(dmat - m_q)).astype(BF16)
    g = jnp.exp(b_col + m - m_q)
    tot = g * _dot(qc, s_old.astype(BF16)) + _dot(s, v_aug)
    num = tot[:, :HEAD_DIM]
    den = tot[:, HEAD_DIM:HEAD_DIM + 1]
    return m_new, num / jnp.maximum(jnp.abs(den), jnp.exp(-m_q))


def _mlstm_kernel(q_ref, k_ref, v_ref, o_ref, gt_ref, ck_ref, cv_ref, cgt_ref, mh_ref, out_ref,
                  hbuf, sf_ref, sb_ref, cum_ref, ccum_ref):
    nc = q_ref.shape[0] // CHUNK
    ncc = ck_ref.shape[0] // CHUNK

    for g_ref, c_ref in ((gt_ref, cum_ref), (cgt_ref, ccum_ref)):
        c_ref[0] = _chunk_cumsum(g_ref[2], False)
        c_ref[1] = _chunk_cumsum(g_ref[3], True)

    sf_ref[...] = jnp.zeros_like(sf_ref)
    sb_ref[...] = jnp.zeros_like(sb_ref)
    m_f = jnp.zeros((1, 1), F32)
    m_b = jnp.zeros((1, 1), F32)

    def step(c, reverse, s_ref, m, refs, with_q):
        kr, vr, gr, cr, qr = refs
        tok = _chunk_rows(c)
        d = 1 if reverse else 0
        i_row = gr[d, :, tok]
        b_row = cr[d, :, tok]
        qc = qr[tok, :] if with_q else None
        return _mlstm_chunk(kr[tok, :], vr[tok, :], i_row, b_row, s_ref, m, reverse, qc)

    ctx_refs = (ck_ref, cv_ref, cgt_ref, ccum_ref, None)
    for c in range(ncc):
        m_f, _ = step(c, False, sf_ref, m_f, ctx_refs, False)
        m_b, _ = step(ncc - 1 - c, True, sb_ref, m_b, ctx_refs, False)

    lat_refs = (k_ref, v_ref, gt_ref, cum_ref, q_ref)

    def first_half(i, carry):
        m_f, m_b = carry
        cb = nc - 1 - i
        m_f, h_f = step(i, False, sf_ref, m_f, lat_refs, True)
        m_b, h_b = step(cb, True, sb_ref, m_b, lat_refs, True)
        hbuf[_chunk_rows(i), :] = h_f
        hbuf[_chunk_rows(cb), :] = h_b
        return m_f, m_b

    def finish(c, h):
        tok = _chunk_rows(c)
        hs = h + hbuf[tok, :]
        hn = _rms(hs) * mh_ref[...]
        out_ref[tok, :] = (jax.nn.sigmoid(o_ref[tok, :].astype(F32)) * hn).astype(out_ref.dtype)

    def second_half(i, carry):
        m_f, m_b = carry
        cb = nc - 1 - i
        m_f, h_f = step(i, False, sf_ref, m_f, lat_refs, True)
        m_b, h_b = step(cb, True, sb_ref, m_b, lat_refs, True)
        finish(i, h_f)
        finish(cb, h_b)
        return m_f, m_b

    carry = lax.fori_loop(0, nc // 2, first_half, (m_f, m_b))
    lax.fori_loop(nc // 2, nc, second_half, carry)


def _mlstm(qkvo, gt, ckv, cgt, mh):
    nb, seq, _ = qkvo.shape
    ctx_len = ckv.shape[1]
    assert seq % (2 * CHUNK) == 0 and ctx_len % CHUNK == 0
    colblk = lambda j: pl.BlockSpec((None, seq, HEAD_DIM), lambda b, h: (b, 0, j * N_HEADS + h))
    cblk = lambda j: pl.BlockSpec((None, ctx_len, HEAD_DIM), lambda b, h: (b, 0, j * N_HEADS + h))
    gblk = lambda n: pl.BlockSpec((None, 4, None, 1, n), lambda b, h: (b, 0, h, 0, 0))
    gview = lambda g: g.reshape(nb, 4, N_HEADS, 1, g.shape[-1])
    return pl.pallas_call(
        _mlstm_kernel,
        grid=(nb, N_HEADS),
        in_specs=[colblk(0), colblk(1), colblk(2), colblk(3),
                  gblk(seq), cblk(0), cblk(1), gblk(ctx_len),
                  pl.BlockSpec((1, HEAD_DIM), lambda b, h: (0, h))],
        out_specs=pl.BlockSpec((None, seq, HEAD_DIM), lambda b, h: (b, 0, h)),
        out_shape=jax.ShapeDtypeStruct((nb, seq, MLSTM_W), BF16),
        scratch_shapes=[pltpu.VMEM((seq, HEAD_DIM), F32),
                        pltpu.VMEM((HEAD_DIM, 2 * HEAD_DIM), F32),
                        pltpu.VMEM((HEAD_DIM, 2 * HEAD_DIM), F32),
                        pltpu.VMEM((2, 1, seq), F32),
                        pltpu.VMEM((2, 1, ctx_len), F32)],
        compiler_params=pltpu.CompilerParams(dimension_semantics=("arbitrary", "arbitrary"),
                                             vmem_limit_bytes=VMEM_LIMIT),
        name="mlstm",
    )(qkvo, qkvo, qkvo, qkvo, gview(gt), ckv, ckv, gview(cgt), mh)


def _out_ffn_kernel(x1_ref, conv_ref, mh_ref, mod_ref, g_ref, wo_ref, wup_ref, wdn_ref, out_ref):
    y = _dot(conv_ref[...], wo_ref[0:CONV_W, :]) + _dot(mh_ref[...], wo_ref[CONV_W:, :])
    x2 = x1_ref[...] + mod_ref[5] * (_rms(y) * g_ref[3])
    out_ref[...] = _ffn_tile(x2, g_ref[4], g_ref[5], mod_ref[6], mod_ref[7], mod_ref[8], wup_ref, wdn_ref)


def _out_ffn(x1, conv, mh, mod4, g3, wo, wup, wdn, seq, tm):
    n, d = x1.shape
    tpb = seq // tm
    tile = lambda w: pl.BlockSpec((tm, w), lambda i: (i, 0))
    return pl.pallas_call(
        _out_ffn_kernel,
        grid=(n // tm,),
        in_specs=[tile(d), tile(CONV_W), tile(MLSTM_W),
                  pl.BlockSpec((None, N_MOD, 1, d), lambda i: (i // tpb, 0, 0, 0)),
                  _resident(g3.shape), _resident(wo.shape), _resident(wup.shape), _resident(wdn.shape)],
        out_specs=tile(d),
        out_shape=jax.ShapeDtypeStruct((n, d), F32),
        compiler_params=pltpu.CompilerParams(dimension_semantics=("arbitrary",),
                                             vmem_limit_bytes=VMEM_LIMIT),
        name="outproj_ffn2",
    )(x1, conv, mh, mod4, g3, wo, wup, wdn)


def kernel(x, c, ctx, c_ctx, w_mod, b_mod, norm_g, ffn1_up, ffn1_down, ffn2_up, ffn2_down,
           w_in, b_gates, conv_w, mh_norm, w_out):
    nb, seq, d = x.shape
    ctx_len = ctx.shape[1]
    depth = w_mod.shape[0]
    assert depth == 1, "only the single (last) layer configuration is implemented"
    assert nb + 1 <= MOD_ROWS and seq % GRID_W == 0
    tm = min(512, seq)
    conv_cols = 3 * CONV_W

    cvec = jnp.concatenate([c, c_ctx[None], jnp.zeros((MOD_ROWS - nb - 1, d), F32)], axis=0)
    mod4 = _modulation(cvec, w_mod[0], b_mod[0]).reshape(MOD_ROWS, N_MOD, 1, d)

    g3 = norm_g[0].reshape(6, 1, d)
    w1u, w1d = ffn1_up[0].astype(BF16), ffn1_down[0].astype(BF16)
    w2u, w2d = ffn2_up[0].astype(BF16), ffn2_down[0].astype(BF16)
    win = w_in[0]
    wc = win[:, :conv_cols].astype(BF16)
    wq = win[:, conv_cols:conv_cols + 4 * MLSTM_W].astype(BF16)
    wkv = win[:, conv_cols + MLSTM_W:conv_cols + 3 * MLSTM_W].astype(BF16)
    wg = win[:, conv_cols + 4 * MLSTM_W:].T.astype(BF16)
    bg = b_gates[0].reshape(N_GATES, 1)
    cw3 = conv_w[0].reshape(3, 1, CONV_W)
    wo = w_out[0].astype(BF16)
    mh = mh_norm[0].reshape(1, MLSTM_W)

    x1, conv, qkvo, gt = _ffn_proj(x.reshape(nb * seq, d), mod4, g3, w1u, w1d, wc, wq, wg, bg, cw3, seq, tm)
    ckv, cgt = _ffn_proj_ctx(ctx.reshape(nb * ctx_len, d), mod4, g3, w1u, w1d, wkv, wg, bg, ctx_len, nb)
    hm = _mlstm(qkvo.reshape(nb, seq, 4 * MLSTM_W), gt, ckv.reshape(nb, ctx_len, 2 * MLSTM_W), cgt, mh)
    out = _out_ffn(x1, conv, hm.reshape(nb * seq, MLSTM_W), mod4, g3, wo, w2u, w2d, seq, tm)
    return out.reshape(nb, seq, d)
```

```python
import jax
import jax.numpy as jnp
from jax import lax
from jax.experimental import pallas as pl
from jax.experimental.pallas import tpu as pltpu

F32 = jnp.float32
BF16 = jnp.bfloat16

GRID_W = 64
CONV_W = 512
N_HEADS = 4
HEAD_DIM = 128
MLSTM_W = N_HEADS * HEAD_DIM
CHUNK = 128
N_MOD = 9
EPS = 1e-6
N_GATES = 4 * N_HEADS
MOD_ROWS = 24
VMEM_LIMIT = 56 * 1024 * 1024

FFN_CHUNK = 512

ROW_A, ROW_M, ROW_E, ROW_W, ROW_DECAY, ROW_MIN, N_ROWS = 0, 1, 2, 3, 4, 5, 8


def _dot(a, b):
    return jnp.dot(a, b, preferred_element_type=F32)


def _dot_nt(a, b):
    return lax.dot_general(a, b, (((1,), (1,)), ((), ())), preferred_element_type=F32)


def _rms(x):
    return x * lax.rsqrt(jnp.mean(x * x, axis=-1, keepdims=True) + EPS)


def _resident(shape):
    nd = len(shape)
    return pl.BlockSpec(shape, lambda *_: (0,) * nd, pipeline_mode=pl.Buffered(1))


def _mod_kernel(c_ref, w_ref, b_ref, o_ref):
    cv = c_ref[...]
    s = (cv * jax.nn.sigmoid(cv)).astype(BF16)
    o_ref[...] = _dot(s, w_ref[...].astype(BF16)) + b_ref[...]


def _modulation(cvec, w_mod, b_mod):
    d = cvec.shape[1]
    n = w_mod.shape[1]
    bn = n // N_MOD
    return pl.pallas_call(
        _mod_kernel,
        grid=(N_MOD,),
        in_specs=[pl.BlockSpec((MOD_ROWS, d), lambda j: (0, 0)),
                  pl.BlockSpec((d, bn), lambda j: (0, j)),
                  pl.BlockSpec((1, bn), lambda j: (0, j))],
        out_specs=pl.BlockSpec((MOD_ROWS, bn), lambda j: (0, j)),
        out_shape=jax.ShapeDtypeStruct((MOD_ROWS, n), F32),
        compiler_params=pltpu.CompilerParams(dimension_semantics=("arbitrary",),
                                             vmem_limit_bytes=VMEM_LIMIT),
        name="modulation",
    )(cvec, w_mod, b_mod.reshape(1, n))


def _ffn_tile(x, g_pre, g_post, shift, scale, gate, wup_ref, wdn_ref):
    ffn = wdn_ref.shape[0]
    h = (_rms(x) * g_pre * (1.0 + scale) + shift).astype(BF16)
    acc = None
    for c0 in range(0, ffn, FFN_CHUNK):
        cw = min(FFN_CHUNK, ffn - c0)
        a = _dot(h, wup_ref[:, c0:c0 + cw])
        b = _dot(h, wup_ref[:, ffn + c0:ffn + c0 + cw])
        act = (a * jax.nn.sigmoid(a) * b).astype(BF16)
        part = _dot(act, wdn_ref[c0:c0 + cw, :])
        acc = part if acc is None else acc + part
    return x + 0.5 * gate * (_rms(acc) * g_post)


def _log_sigmoid(x):
    return jnp.minimum(x, 0.0) - jnp.log1p(jnp.exp(-jnp.abs(x)))


def _gates_t(h, wg_ref, bg_ref):
    g = _dot_nt(wg_ref[...], h) + bg_ref[...]
    half = N_GATES // 2
    return jnp.concatenate([g[:half], _log_sigmoid(g[half:])], axis=0)


def _mixer_in(x1, g_ref, mod_ref):
    return (_rms(x1) * g_ref[2] * (1.0 + mod_ref[4]) + mod_ref[3]).astype(BF16)


def _ffn_proj_kernel(x_ref, mod_ref, g_ref, wup_ref, wdn_ref, wc_ref, wqvo_ref, wkt_ref, wg_ref, bg_ref,
                     cw_ref, x1_ref, conv_ref, qvo_ref, kt_ref, gt_ref):
    x1 = _ffn_tile(x_ref[...], g_ref[0], g_ref[1], mod_ref[0], mod_ref[1], mod_ref[2], wup_ref, wdn_ref)
    x1_ref[...] = x1
    h = _mixer_in(x1, g_ref, mod_ref)
    tm = h.shape[0]

    bg = _dot(h, wc_ref[:, 0:CONV_W])
    cu = _dot(h, wc_ref[:, CONV_W:2 * CONV_W]) * _dot(h, wc_ref[:, 2 * CONV_W:3 * CONV_W])
    col = lax.broadcasted_iota(jnp.int32, (tm, 1), 0) % GRID_W
    prev = jnp.where(col == 0, 0.0, pltpu.roll(cu, 1, 0))
    nxt = jnp.where(col == GRID_W - 1, 0.0, pltpu.roll(cu, tm - 1, 0))
    conv = bg * (cw_ref[0] * prev + cw_ref[1] * cu + cw_ref[2] * nxt)
    conv_ref[...] = conv.astype(BF16)

    for j in range(3):
        cols = slice(j * MLSTM_W, (j + 1) * MLSTM_W)
        qvo_ref[:, cols] = _dot(h, wqvo_ref[:, cols]).astype(BF16)
    kt_ref[...] = (_dot_nt(wkt_ref[...], h) * (HEAD_DIM ** -0.5)).astype(BF16)
    gt_ref[...] = _gates_t(h, wg_ref, bg_ref)


def _ffn_proj(x2d, mod4, g3, wup, wdn, wc, wqvo, wkt, wg, bg, cw3, seq, tm):
    n, d = x2d.shape
    tpb = seq // tm
    nb = n // seq
    tile = lambda w: pl.BlockSpec((tm, w), lambda i: (i, 0))
    tile_t = lambda r: pl.BlockSpec((None, r, tm), lambda i: (i // tpb, 0, i % tpb))
    return pl.pallas_call(
        _ffn_proj_kernel,
        grid=(n // tm,),
        in_specs=[tile(d),
                  pl.BlockSpec((None, N_MOD, 1, d), lambda i: (i // tpb, 0, 0, 0)),
                  _resident(g3.shape), _resident(wup.shape), _resident(wdn.shape),
                  _resident(wc.shape), _resident(wqvo.shape), _resident(wkt.shape), _resident(wg.shape),
                  _resident(bg.shape), _resident(cw3.shape)],
        out_specs=[tile(d), tile(CONV_W), tile(3 * MLSTM_W), tile_t(MLSTM_W), tile_t(N_GATES)],
        out_shape=[jax.ShapeDtypeStruct((n, d), F32),
                   jax.ShapeDtypeStruct((n, CONV_W), BF16),
                   jax.ShapeDtypeStruct((n, 3 * MLSTM_W), BF16),
                   jax.ShapeDtypeStruct((nb, MLSTM_W, seq), BF16),
                   jax.ShapeDtypeStruct((nb, N_GATES, seq), F32)],
        compiler_params=pltpu.CompilerParams(dimension_semantics=("arbitrary",),
                                             vmem_limit_bytes=VMEM_LIMIT),
        name="ffn1_inproj",
    )(x2d, mod4, g3, wup, wdn, wc, wqvo, wkt, wg, bg, cw3)


def _ffn_proj_ctx_kernel(x_ref, mod_ref, g_ref, wup_ref, wdn_ref, wv_ref, wkt_ref, wg_ref, bg_ref,
                         v_ref, kt_ref, gt_ref):
    x1 = _ffn_tile(x_ref[...], g_ref[0], g_ref[1], mod_ref[0], mod_ref[1], mod_ref[2], wup_ref, wdn_ref)
    h = _mixer_in(x1, g_ref, mod_ref)
    v_ref[...] = _dot(h, wv_ref[...]).astype(BF16)
    kt_ref[...] = (_dot_nt(wkt_ref[...], h) * (HEAD_DIM ** -0.5)).astype(BF16)
    gt_ref[...] = _gates_t(h, wg_ref, bg_ref)


def _ffn_proj_ctx(c2d, mod4, g3, wup, wdn, wv, wkt, wg, bg, ctx_len, ctx_row):
    n, d = c2d.shape
    tm = ctx_len
    tile_t = lambda r: pl.BlockSpec((None, r, tm), lambda i: (i, 0, 0))
    return pl.pallas_call(
        _ffn_proj_ctx_kernel,
        grid=(n // tm,),
        in_specs=[pl.BlockSpec((tm, d), lambda i: (i, 0)),
                  pl.BlockSpec((None, N_MOD, 1, d), lambda i: (ctx_row, 0, 0, 0)),
                  _resident(g3.shape), _resident(wup.shape), _resident(wdn.shape),
                  _resident(wv.shape), _resident(wkt.shape), _resident(wg.shape), _resident(bg.shape)],
        out_specs=[pl.BlockSpec((tm, MLSTM_W), lambda i: (i, 0)), tile_t(MLSTM_W), tile_t(N_GATES)],
        out_shape=[jax.ShapeDtypeStruct((n, MLSTM_W), BF16),
                   jax.ShapeDtypeStruct((n // tm, MLSTM_W, tm), BF16),
                   jax.ShapeDtypeStruct((n // tm, N_GATES, tm), F32)],
        compiler_params=pltpu.CompilerParams(dimension_semantics=("arbitrary",),
                                             vmem_limit_bytes=VMEM_LIMIT),
        name="ffn1_inproj_ctx",
    )(c2d, mod4, g3, wup, wdn, wv, wkt, wg, bg)


def _lane_scan(x, op, reverse):
    pos = lax.broadcasted_iota(jnp.int32, x.shape, 1)
    d = 1
    while d < CHUNK:
        if reverse:
            x = jnp.where(pos < CHUNK - d, op(x, pltpu.roll(x, CHUNK - d, 1)), x)
        else:
            x = jnp.where(pos >= d, op(x, pltpu.roll(x, d, 1)), x)
        d *= 2
    return x


def _lane_pick(x, lane):
    idx = lax.broadcasted_iota(jnp.int32, x.shape, 1)
    return jnp.sum(jnp.where(idx == lane, x, 0.0), axis=1, keepdims=True)


def _chunk_rows(c):
    if isinstance(c, int):
        return pl.ds(c * CHUNK, CHUNK)
    return pl.ds(pl.multiple_of(c * CHUNK, CHUNK), CHUNK)


def _gate_scan(ig, lf, m, reverse):
    n = ig.shape[0]
    b = _lane_scan(lf, jnp.add, reverse)
    a = ig - b
    b_last = _lane_pick(b, 0 if reverse else CHUNK - 1)
    wmax = b_last + jnp.max(a, axis=1, keepdims=True)
    sub = lax.broadcasted_iota(jnp.int32, (n, 1), 0)
    m_in = jnp.zeros((n, 1), F32)
    m_out = jnp.zeros((n, 1), F32)
    for c in (reversed(range(n)) if reverse else range(n)):
        m_new = jnp.maximum(b_last[c:c + 1] + m, wmax[c:c + 1])
        m_in = jnp.where(sub == c, m, m_in)
        m_out = jnp.where(sub == c, m_new, m_out)
        m = m_new
    big_m = jnp.maximum(m_in, _lane_scan(a, jnp.maximum, reverse))
    e = jnp.exp(-(b + big_m))
    w = jnp.exp(a + b_last - m_out)
    decay = jnp.exp(b_last + m_in - m_out)
    return (a, big_m, e, w, decay, m_in), m


def _v_aug(v):
    return jnp.concatenate([v, jnp.ones((CHUNK, HEAD_DIM), BF16)], axis=1)


def _mlstm_kernel(qvo_q, kt_ref, qvo_v, qvo_o, g_ref, ckt_ref, cv_ref, cg_ref, mh_ref, out_ref,
                  hbuf, s_ref, rows_ref):
    nc = kt_ref.shape[1] // CHUNK
    ncc = ckt_ref.shape[1] // CHUNK
    q_ref, v_ref, o_ref = qvo_q, qvo_v, qvo_o

    for h in range(N_HEADS):
        hs = slice(h * HEAD_DIM, (h + 1) * HEAD_DIM)
        for d in range(2):
            hd = 2 * h + d
            rev = d == 1
            i_row, f_row = d * N_HEADS + h, (2 + d) * N_HEADS + h
            (_, _, _, w, decay, _), m = _gate_scan(cg_ref[i_row], cg_ref[f_row], jnp.zeros((1, 1), F32), rev)
            s = jnp.zeros((HEAD_DIM, 2 * HEAD_DIM), F32)
            for c in (reversed(range(ncc)) if rev else range(ncc)):
                tok = _chunk_rows(c)
                kw = (ckt_ref[hs, tok].astype(F32) * w[c:c + 1]).astype(BF16)
                s = decay[c:c + 1] * s + _dot(kw, _v_aug(cv_ref[tok, hs]))
            s_ref[hd] = s
            (a, big_m, e, w, decay, m_in), _ = _gate_scan(g_ref[i_row], g_ref[f_row], m, rev)
            decay = jnp.broadcast_to(decay, (nc, CHUNK))
            m_in = jnp.broadcast_to(m_in, (nc, CHUNK))
            for c in range(nc):
                for r, val in ((ROW_A, a), (ROW_M, big_m), (ROW_E, e), (ROW_W, w),
                               (ROW_DECAY, decay), (ROW_MIN, m_in)):
                    rows_ref[hd, c, r:r + 1, :] = val[c:c + 1]

    row_id = lax.broadcasted_iota(jnp.int32, (CHUNK, CHUNK), 0)
    col_id = lax.broadcasted_iota(jnp.int32, (CHUNK, CHUNK), 1)

    def to_col(row):
        return jnp.sum(jnp.where(row_id == col_id, row, 0.0), axis=1, keepdims=True)

    def scores(h, d, c):
        hs = slice(h * HEAD_DIM, (h + 1) * HEAD_DIM)
        tok = _chunk_rows(c)
        rows = rows_ref[2 * h + d, c]
        m_col = to_col(rows[ROW_M:ROW_M + 1])
        g_col = jnp.exp(rows[ROW_MIN:ROW_MIN + 1] - m_col)
        mask = (col_id >= row_id) if d == 1 else (col_id <= row_id)
        dmat = jnp.exp(jnp.where(mask, rows[ROW_A:ROW_A + 1] - m_col, -jnp.inf))
        q = q_ref[tok, hs]
        kt = kt_ref[hs, tok]
        lhs = jnp.concatenate([(_dot(q, kt) * dmat).astype(BF16),
                               (q.astype(F32) * g_col).astype(BF16)], axis=1)
        kw = (kt.astype(F32) * rows[ROW_W:ROW_W + 1]).astype(BF16)
        return lhs, kw, rows

    def readout(h, d, c, lhs, kw, rows):
        hs = slice(h * HEAD_DIM, (h + 1) * HEAD_DIM)
        v_aug = _v_aug(v_ref[_chunk_rows(c), hs])
        s_old = s_ref[2 * h + d]
        tot = _dot(lhs, jnp.concatenate([v_aug, s_old.astype(BF16)], axis=0))
        dec = rows[ROW_DECAY:ROW_DECAY + 1]
        s_ref[2 * h + d] = jnp.concatenate([dec, dec], axis=1) * s_old + _dot(kw, v_aug)
        return tot

    def normalise(tot, rows):
        e_col = to_col(rows[ROW_E:ROW_E + 1])
        return tot[:, :HEAD_DIM] / jnp.maximum(jnp.abs(tot[:, HEAD_DIM:]), e_col)

    def finish(h, c, hh):
        hs = slice(h * HEAD_DIM, (h + 1) * HEAD_DIM)
        tok = _chunk_rows(c)
        hn = _rms(hh + hbuf[tok, hs]) * mh_ref[:, hs]
        out_ref[tok, hs] = (jax.nn.sigmoid(o_ref[tok, hs].astype(F32)) * hn).astype(out_ref.dtype)

    def iteration(i, last):
        scans = [(h, d, (nc - 1 - i) if d else i) for h in range(N_HEADS) for d in range(2)]
        staged = [scores(*s) for s in scans]
        tots = [readout(*s, *st) for s, st in zip(scans, staged)]
        for (h, d, c), tot, st in zip(scans, tots, staged):
            hh = normalise(tot, st[2])
            if last:
                finish(h, c, hh)
            else:
                hbuf[_chunk_rows(c), slice(h * HEAD_DIM, (h + 1) * HEAD_DIM)] = hh

    def first_half(i, carry):
        iteration(i, False)
        return carry

    def second_half(i, carry):
        iteration(i, True)
        return carry

    lax.fori_loop(0, nc // 2, first_half, 0)
    lax.fori_loop(nc // 2, nc, second_half, 0)


def _mlstm(qvo, kt, gt, cv, ckt, cgt, mh):
    nb, seq, _ = qvo.shape
    ctx_len = cv.shape[1]
    assert seq % (2 * CHUNK) == 0 and ctx_len % CHUNK == 0
    nc, ncc = seq // CHUNK, ctx_len // CHUNK
    qvo_blk = lambda j: pl.BlockSpec((None, seq, MLSTM_W), lambda b: (b, 0, j))
    whole = lambda *s: pl.BlockSpec((None,) + s, lambda b: (b,) + (0,) * len(s))
    return pl.pallas_call(
        _mlstm_kernel,
        grid=(nb,),
        in_specs=[qvo_blk(0), whole(MLSTM_W, seq), qvo_blk(1), qvo_blk(2), whole(N_GATES, nc, CHUNK),
                  whole(MLSTM_W, ctx_len), whole(ctx_len, MLSTM_W), whole(N_GATES, ncc, CHUNK),
                  pl.BlockSpec((1, MLSTM_W), lambda b: (0, 0))],
        out_specs=whole(seq, MLSTM_W),
        out_shape=jax.ShapeDtypeStruct((nb, seq, MLSTM_W), BF16),
        scratch_shapes=[pltpu.VMEM((seq, MLSTM_W), F32),
                        pltpu.VMEM((2 * N_HEADS, HEAD_DIM, 2 * HEAD_DIM), F32),
                        pltpu.VMEM((2 * N_HEADS, nc, N_ROWS, CHUNK), F32)],
        compiler_params=pltpu.CompilerParams(dimension_semantics=("arbitrary",),
                                             vmem_limit_bytes=VMEM_LIMIT),
        name="mlstm",
    )(qvo, kt, qvo, qvo, gt.reshape(nb, N_GATES, nc, CHUNK), ckt, cv,
      cgt.reshape(nb, N_GATES, ncc, CHUNK), mh)


def _out_ffn_kernel(x1_ref, conv_ref, mh_ref, mod_ref, g_ref, wo_ref, wup_ref, wdn_ref, out_ref):
    y = _dot(conv_ref[...], wo_ref[0:CONV_W, :]) + _dot(mh_ref[...], wo_ref[CONV_W:, :])
    x2 = x1_ref[...] + mod_ref[5] * (_rms(y) * g_ref[3])
    out_ref[...] = _ffn_tile(x2, g_ref[4], g_ref[5], mod_ref[6], mod_ref[7], mod_ref[8], wup_ref, wdn_ref)


def _out_ffn(x1, conv, mh, mod4, g3, wo, wup, wdn, seq, tm):
    n, d = x1.shape
    tpb = seq // tm
    tile = lambda w: pl.BlockSpec((tm, w), lambda i: (i, 0))
    return pl.pallas_call(
        _out_ffn_kernel,
        grid=(n // tm,),
        in_specs=[tile(d), tile(CONV_W), tile(MLSTM_W),
                  pl.BlockSpec((None, N_MOD, 1, d), lambda i: (i // tpb, 0, 0, 0)),
                  _resident(g3.shape), _resident(wo.shape), _resident(wup.shape), _resident(wdn.shape)],
        out_specs=tile(d),
        out_shape=jax.ShapeDtypeStruct((n, d), F32),
        compiler_params=pltpu.CompilerParams(dimension_semantics=("arbitrary",),
                                             vmem_limit_bytes=VMEM_LIMIT),
        name="outproj_ffn2",
    )(x1, conv, mh, mod4, g3, wo, wup, wdn)


def kernel(x, c, ctx, c_ctx, w_mod, b_mod, norm_g, ffn1_up, ffn1_down, ffn2_up, ffn2_down,
           w_in, b_gates, conv_w, mh_norm, w_out):
    nb, seq, d = x.shape
    ctx_len = ctx.shape[1]
    depth = w_mod.shape[0]
    assert depth == 1, "only the single (last) layer configuration is implemented"
    assert nb + 1 <= MOD_ROWS and seq % GRID_W == 0
    tm = min(512, seq)
    conv_cols = 3 * CONV_W

    cvec = jnp.concatenate([c, c_ctx[None], jnp.zeros((MOD_ROWS - nb - 1, d), F32)], axis=0)
    mod4 = _modulation(cvec, w_mod[0], b_mod[0]).reshape(MOD_ROWS, N_MOD, 1, d)

    g3 = norm_g[0].reshape(6, 1, d)
    w1u, w1d = ffn1_up[0].astype(BF16), ffn1_down[0].astype(BF16)
    w2u, w2d = ffn2_up[0].astype(BF16), ffn2_down[0].astype(BF16)
    win = w_in[0]
    seg = lambda j: win[:, conv_cols + j * MLSTM_W:conv_cols + (j + 1) * MLSTM_W]
    wc = win[:, :conv_cols].astype(BF16)
    wqvo = jnp.concatenate([seg(0), seg(2), seg(3)], axis=1).astype(BF16)
    wv = seg(2).astype(BF16)
    wkt = seg(1).T.astype(BF16)
    wg = win[:, conv_cols + 4 * MLSTM_W:].T.astype(BF16)
    bg = b_gates[0].reshape(N_GATES, 1)
    cw3 = conv_w[0].reshape(3, 1, CONV_W)
    wo = w_out[0].astype(BF16)
    mh = mh_norm[0].reshape(1, MLSTM_W)

    x1, conv, qvo, kt, gt = _ffn_proj(x.reshape(nb * seq, d), mod4, g3, w1u, w1d, wc, wqvo, wkt, wg, bg,
                                      cw3, seq, tm)
    cv, ckt, cgt = _ffn_proj_ctx(ctx.reshape(nb * ctx_len, d), mod4, g3, w1u, w1d, wv, wkt, wg, bg,
                                 ctx_len, nb)
    hm = _mlstm(qvo.reshape(nb, seq, 3 * MLSTM_W), kt, gt, cv.reshape(nb, ctx_len, MLSTM_W), ckt, cgt, mh)
    out = _out_ffn(x1, conv, hm.reshape(nb * seq, MLSTM_W), mod4, g3, wo, w2u, w2d, seq, tm)
    return out.reshape(nb, seq, d)
```

```python
import jax
import jax.numpy as jnp
from jax import lax
from jax.experimental import pallas as pl
from jax.experimental.pallas import tpu as pltpu

F32 = jnp.float32
BF16 = jnp.bfloat16

GRID_W = 64
CONV_W = 512
N_HEADS = 4
HEAD_DIM = 128
MLSTM_W = N_HEADS * HEAD_DIM
CHUNK = 128
N_MOD = 9
EPS = 1e-6
N_GATES = 4 * N_HEADS
MOD_ROWS = 24
VMEM_LIMIT = 56 * 1024 * 1024

FFN_CHUNK = 512
N_SUB = 2

ROW_A, ROW_M, ROW_E, ROW_W, ROW_DECAY, ROW_MIN, N_ROWS = 0, 1, 2, 3, 4, 5, 8


def _dot(a, b):
    return jnp.dot(a, b, preferred_element_type=F32)


def _dot_nt(a, b):
    return lax.dot_general(a, b, (((1,), (1,)), ((), ())), preferred_element_type=F32)


def _rms(x):
    return x * lax.rsqrt(jnp.mean(x * x, axis=-1, keepdims=True) + EPS)


def _resident(shape):
    nd = len(shape)
    return pl.BlockSpec(shape, lambda *_: (0,) * nd, pipeline_mode=pl.Buffered(1))


def _mod_kernel(c_ref, w_ref, b_ref, o_ref):
    cv = c_ref[...]
    s = (cv * jax.nn.sigmoid(cv)).astype(BF16)
    o_ref[...] = _dot(s, w_ref[...].astype(BF16)) + b_ref[...]


def _modulation(cvec, w_mod, b_mod):
    d = cvec.shape[1]
    n = w_mod.shape[1]
    bn = n // N_MOD
    return pl.pallas_call(
        _mod_kernel,
        grid=(N_MOD,),
        in_specs=[pl.BlockSpec((MOD_ROWS, d), lambda j: (0, 0)),
                  pl.BlockSpec((d, bn), lambda j: (0, j)),
                  pl.BlockSpec((1, bn), lambda j: (0, j))],
        out_specs=pl.BlockSpec((MOD_ROWS, bn), lambda j: (0, j)),
        out_shape=jax.ShapeDtypeStruct((MOD_ROWS, n), F32),
        compiler_params=pltpu.CompilerParams(dimension_semantics=("arbitrary",),
                                             vmem_limit_bytes=VMEM_LIMIT),
        name="modulation",
    )(cvec, w_mod, b_mod.reshape(1, n))


def _ffn_in(x, g_pre, shift, scale):
    return (_rms(x) * g_pre * (1.0 + scale) + shift).astype(BF16)


def _swiglu(h, wup_ref, wdn_ref):
    ffn = wdn_ref.shape[0]
    chunks = [(c0, min(FFN_CHUNK, ffn - c0)) for c0 in range(0, ffn, FFN_CHUNK)]

    def up(c0, cw):
        return _dot(h, wup_ref[:, c0:c0 + cw]), _dot(h, wup_ref[:, ffn + c0:ffn + c0 + cw])

    acc = None
    ab = up(*chunks[0])
    for i, (c0, cw) in enumerate(chunks):
        a, b = ab
        if i + 1 < len(chunks):
            ab = up(*chunks[i + 1])
        act = (a * jax.nn.sigmoid(a) * b).astype(BF16)
        part = _dot(act, wdn_ref[c0:c0 + cw, :])
        acc = part if acc is None else acc + part
    return acc


def _ffn_out(x, acc, g_post, gate):
    return x + 0.5 * gate * (_rms(acc) * g_post)


def _sub_rows(n_rows):
    sub = n_rows // N_SUB
    return [pl.ds(r * sub, sub) for r in range(N_SUB)]


def _log_sigmoid(x):
    return jnp.minimum(x, 0.0) - jnp.log1p(jnp.exp(-jnp.abs(x)))


def _gates_t(h, wg_ref, bg_ref):
    g = _dot_nt(wg_ref[...], h) + bg_ref[...]
    half = N_GATES // 2
    return jnp.concatenate([g[:half], _log_sigmoid(g[half:])], axis=0)


def _mixer_in(x1, g_ref, mod_ref):
    return (_rms(x1) * g_ref[2] * (1.0 + mod_ref[4]) + mod_ref[3]).astype(BF16)


def _ffn_proj_kernel(x_ref, mod_ref, g_ref, wup_ref, wdn_ref, wc_ref, wqvo_ref, wkt_ref, wg_ref, bg_ref,
                     cw_ref, x1_ref, conv_ref, qvo_ref, kt_ref, gt_ref):
    rows = _sub_rows(x_ref.shape[0])
    hs = [_ffn_in(x_ref[r, :], g_ref[0], mod_ref[0], mod_ref[1]) for r in rows]
    accs = [_swiglu(h, wup_ref, wdn_ref) for h in hs]
    h2s = []
    for r, acc in zip(rows, accs):
        x1 = _ffn_out(x_ref[r, :], acc, g_ref[1], mod_ref[2])
        x1_ref[r, :] = x1
        h2s.append(_mixer_in(x1, g_ref, mod_ref))

    for r, h in zip(rows, h2s):
        sub = h.shape[0]
        bg = _dot(h, wc_ref[:, 0:CONV_W])
        cu = _dot(h, wc_ref[:, CONV_W:2 * CONV_W]) * _dot(h, wc_ref[:, 2 * CONV_W:3 * CONV_W])
        col = lax.broadcasted_iota(jnp.int32, (sub, 1), 0) % GRID_W
        prev = jnp.where(col == 0, 0.0, pltpu.roll(cu, 1, 0))
        nxt = jnp.where(col == GRID_W - 1, 0.0, pltpu.roll(cu, sub - 1, 0))
        conv = bg * (cw_ref[0] * prev + cw_ref[1] * cu + cw_ref[2] * nxt)
        conv_ref[r, :] = conv.astype(BF16)

        for j in range(3):
            cols = slice(j * MLSTM_W, (j + 1) * MLSTM_W)
            qvo_ref[r, cols] = _dot(h, wqvo_ref[:, cols]).astype(BF16)
        kt_ref[:, r] = (_dot_nt(wkt_ref[...], h) * (HEAD_DIM ** -0.5)).astype(BF16)
        gt_ref[:, r] = _gates_t(h, wg_ref, bg_ref)


def _ffn_proj(x2d, mod4, g3, wup, wdn, wc, wqvo, wkt, wg, bg, cw3, seq, tm):
    n, d = x2d.shape
    tpb = seq // tm
    nb = n // seq
    tile = lambda w: pl.BlockSpec((tm, w), lambda i: (i, 0))
    tile_t = lambda r: pl.BlockSpec((None, r, tm), lambda i: (i // tpb, 0, i % tpb))
    return pl.pallas_call(
        _ffn_proj_kernel,
        grid=(n // tm,),
        in_specs=[tile(d),
                  pl.BlockSpec((None, N_MOD, 1, d), lambda i: (i // tpb, 0, 0, 0)),
                  _resident(g3.shape), _resident(wup.shape), _resident(wdn.shape),
                  _resident(wc.shape), _resident(wqvo.shape), _resident(wkt.shape), _resident(wg.shape),
                  _resident(bg.shape), _resident(cw3.shape)],
        out_specs=[tile(d), tile(CONV_W), tile(3 * MLSTM_W), tile_t(MLSTM_W), tile_t(N_GATES)],
        out_shape=[jax.ShapeDtypeStruct((n, d), F32),
                   jax.ShapeDtypeStruct((n, CONV_W), BF16),
                   jax.ShapeDtypeStruct((n, 3 * MLSTM_W), BF16),
                   jax.ShapeDtypeStruct((nb, MLSTM_W, seq), BF16),
                   jax.ShapeDtypeStruct((nb, N_GATES, seq), F32)],
        compiler_params=pltpu.CompilerParams(dimension_semantics=("arbitrary",),
                                             vmem_limit_bytes=VMEM_LIMIT),
        name="ffn1_inproj",
    )(x2d, mod4, g3, wup, wdn, wc, wqvo, wkt, wg, bg, cw3)


def _ffn_proj_ctx_kernel(x_ref, mod_ref, g_ref, wup_ref, wdn_ref, wv_ref, wkt_ref, wg_ref, bg_ref,
                         v_ref, kt_ref, gt_ref):
    n_sub, ctx_len = kt_ref.shape[0], kt_ref.shape[2]
    rows = [pl.ds(r * ctx_len, ctx_len) for r in range(n_sub)]
    hs = [_ffn_in(x_ref[r, :], g_ref[0], mod_ref[0], mod_ref[1]) for r in rows]
    accs = [_swiglu(h, wup_ref, wdn_ref) for h in hs]
    h2s = [_mixer_in(_ffn_out(x_ref[r, :], acc, g_ref[1], mod_ref[2]), g_ref, mod_ref)
           for r, acc in zip(rows, accs)]
    for j, (r, h) in enumerate(zip(rows, h2s)):
        v_ref[r, :] = _dot(h, wv_ref[...]).astype(BF16)
        kt_ref[j] = (_dot_nt(wkt_ref[...], h) * (HEAD_DIM ** -0.5)).astype(BF16)
        gt_ref[j] = _gates_t(h, wg_ref, bg_ref)


def _ffn_proj_ctx(c2d, mod4, g3, wup, wdn, wv, wkt, wg, bg, ctx_len, ctx_row):
    n, d = c2d.shape
    assert (n // ctx_len) % N_SUB == 0
    tm = N_SUB * ctx_len
    tile_t = lambda r: pl.BlockSpec((N_SUB, r, ctx_len), lambda i: (i, 0, 0))
    return pl.pallas_call(
        _ffn_proj_ctx_kernel,
        grid=(n // tm,),
        in_specs=[pl.BlockSpec((tm, d), lambda i: (i, 0)),
                  pl.BlockSpec((None, N_MOD, 1, d), lambda i: (ctx_row, 0, 0, 0)),
                  _resident(g3.shape), _resident(wup.shape), _resident(wdn.shape),
                  _resident(wv.shape), _resident(wkt.shape), _resident(wg.shape), _resident(bg.shape)],
        out_specs=[pl.BlockSpec((tm, MLSTM_W), lambda i: (i, 0)), tile_t(MLSTM_W), tile_t(N_GATES)],
        out_shape=[jax.ShapeDtypeStruct((n, MLSTM_W), BF16),
                   jax.ShapeDtypeStruct((n // ctx_len, MLSTM_W, ctx_len), BF16),
                   jax.ShapeDtypeStruct((n // ctx_len, N_GATES, ctx_len), F32)],
        compiler_params=pltpu.CompilerParams(dimension_semantics=("arbitrary",),
                                             vmem_limit_bytes=VMEM_LIMIT),
        name="ffn1_inproj_ctx",
    )(c2d, mod4, g3, wup, wdn, wv, wkt, wg, bg)


def _lane_scan(x, op, reverse):
    pos = lax.broadcasted_iota(jnp.int32, x.shape, 1)
    d = 1
    while d < CHUNK:
        if reverse:
            x = jnp.where(pos < CHUNK - d, op(x, pltpu.roll(x, CHUNK - d, 1)), x)
        else:
            x = jnp.where(pos >= d, op(x, pltpu.roll(x, d, 1)), x)
        d *= 2
    return x


def _lane_pick(x, lane):
    idx = lax.broadcasted_iota(jnp.int32, x.shape, 1)
    return jnp.sum(jnp.where(idx == lane, x, 0.0), axis=1, keepdims=True)


def _chunk_rows(c):
    if isinstance(c, int):
        return pl.ds(c * CHUNK, CHUNK)
    return pl.ds(pl.multiple_of(c * CHUNK, CHUNK), CHUNK)


def _gate_scan(ig, lf, m, reverse):
    n = ig.shape[0]
    b = _lane_scan(lf, jnp.add, reverse)
    a = ig - b
    b_last = _lane_pick(b, 0 if reverse else CHUNK - 1)
    wmax = b_last + jnp.max(a, axis=1, keepdims=True)
    sub = lax.broadcasted_iota(jnp.int32, (n, 1), 0)
    m_in = jnp.zeros((n, 1), F32)
    m_out = jnp.zeros((n, 1), F32)
    for c in (reversed(range(n)) if reverse else range(n)):
        m_new = jnp.maximum(b_last[c:c + 1] + m, wmax[c:c + 1])
        m_in = jnp.where(sub == c, m, m_in)
        m_out = jnp.where(sub == c, m_new, m_out)
        m = m_new
    big_m = jnp.maximum(m_in, _lane_scan(a, jnp.maximum, reverse))
    e = jnp.exp(-(b + big_m))
    w = jnp.exp(a + b_last - m_out)
    decay = jnp.exp(b_last + m_in - m_out)
    return (a, big_m, e, w, decay, m_in), m


def _v_aug(v):
    return jnp.concatenate([v, jnp.ones((CHUNK, HEAD_DIM), BF16)], axis=1)


def _mlstm_kernel(qvo_q, kt_ref, qvo_v, qvo_o, g_ref, ckt_ref, cv_ref, cg_ref, mh_ref, out_ref,
                  hbuf, s_ref, rows_ref):
    nc = kt_ref.shape[1] // CHUNK
    ncc = ckt_ref.shape[1] // CHUNK
    q_ref, v_ref, o_ref = qvo_q, qvo_v, qvo_o

    for h in range(N_HEADS):
        hs = slice(h * HEAD_DIM, (h + 1) * HEAD_DIM)
        for d in range(2):
            hd = 2 * h + d
            rev = d == 1
            i_row, f_row = d * N_HEADS + h, (2 + d) * N_HEADS + h
            (_, _, _, w, decay, _), m = _gate_scan(cg_ref[i_row], cg_ref[f_row], jnp.zeros((1, 1), F32), rev)
            s = jnp.zeros((HEAD_DIM, 2 * HEAD_DIM), F32)
            for c in (reversed(range(ncc)) if rev else range(ncc)):
                tok = _chunk_rows(c)
                kw = (ckt_ref[hs, tok].astype(F32) * w[c:c + 1]).astype(BF16)
                s = decay[c:c + 1] * s + _dot(kw, _v_aug(cv_ref[tok, hs]))
            s_ref[hd] = s
            (a, big_m, e, w, decay, m_in), _ = _gate_scan(g_ref[i_row], g_ref[f_row], m, rev)
            decay = jnp.broadcast_to(decay, (nc, CHUNK))
            m_in = jnp.broadcast_to(m_in, (nc, CHUNK))
            for c in range(nc):
                for r, val in ((ROW_A, a), (ROW_M, big_m), (ROW_E, e), (ROW_W, w),
                               (ROW_DECAY, decay), (ROW_MIN, m_in)):
                    rows_ref[hd, c, r:r + 1, :] = val[c:c + 1]

    row_id = lax.broadcasted_iota(jnp.int32, (CHUNK, CHUNK), 0)
    col_id = lax.broadcasted_iota(jnp.int32, (CHUNK, CHUNK), 1)

    def to_col(row):
        return jnp.sum(jnp.where(row_id == col_id, row, 0.0), axis=1, keepdims=True)

    def scores(h, d, c):
        hs = slice(h * HEAD_DIM, (h + 1) * HEAD_DIM)
        tok = _chunk_rows(c)
        rows = rows_ref[2 * h + d, c]
        m_col = to_col(rows[ROW_M:ROW_M + 1])
        g_col = jnp.exp(rows[ROW_MIN:ROW_MIN + 1] - m_col)
        mask = (col_id >= row_id) if d == 1 else (col_id <= row_id)
        dmat = jnp.exp(jnp.where(mask, rows[ROW_A:ROW_A + 1] - m_col, -jnp.inf))
        q = q_ref[tok, hs]
        kt = kt_ref[hs, tok]
        lhs = jnp.concatenate([(_dot(q, kt) * dmat).astype(BF16),
                               (q.astype(F32) * g_col).astype(BF16)], axis=1)
        kw = (kt.astype(F32) * rows[ROW_W:ROW_W + 1]).astype(BF16)
        return lhs, kw, rows

    def readout(h, d, c, lhs, kw, rows):
        hs = slice(h * HEAD_DIM, (h + 1) * HEAD_DIM)
        v_aug = _v_aug(v_ref[_chunk_rows(c), hs])
        s_old = s_ref[2 * h + d]
        tot = _dot(lhs, jnp.concatenate([v_aug, s_old.astype(BF16)], axis=0))
        dec = rows[ROW_DECAY:ROW_DECAY + 1]
        s_ref[2 * h + d] = jnp.concatenate([dec, dec], axis=1) * s_old + _dot(kw, v_aug)
        return tot

    def normalise(tot, rows):
        e_col = to_col(rows[ROW_E:ROW_E + 1])
        return tot[:, :HEAD_DIM] / jnp.maximum(jnp.abs(tot[:, HEAD_DIM:]), e_col)

    def finish(h, c, hh):
        hs = slice(h * HEAD_DIM, (h + 1) * HEAD_DIM)
        tok = _chunk_rows(c)
        hn = _rms(hh + hbuf[tok, hs]) * mh_ref[:, hs]
        out_ref[tok, hs] = (jax.nn.sigmoid(o_ref[tok, hs].astype(F32)) * hn).astype(out_ref.dtype)

    def iteration(i, last):
        scans = [(h, d, (nc - 1 - i) if d else i) for h in range(N_HEADS) for d in range(2)]
        staged = [scores(*s) for s in scans]
        tots = [readout(*s, *st) for s, st in zip(scans, staged)]
        for (h, d, c), tot, st in zip(scans, tots, staged):
            hh = normalise(tot, st[2])
            if last:
                finish(h, c, hh)
            else:
                hbuf[_chunk_rows(c), slice(h * HEAD_DIM, (h + 1) * HEAD_DIM)] = hh

    def first_half(i, carry):
        iteration(i, False)
        return carry

    def second_half(i, carry):
        iteration(i, True)
        return carry

    lax.fori_loop(0, nc // 2, first_half, 0)
    lax.fori_loop(nc // 2, nc, second_half, 0)


def _mlstm(qvo, kt, gt, cv, ckt, cgt, mh):
    nb, seq, _ = qvo.shape
    ctx_len = cv.shape[1]
    assert seq % (2 * CHUNK) == 0 and ctx_len % CHUNK == 0
    nc, ncc = seq // CHUNK, ctx_len // CHUNK
    qvo_blk = lambda j: pl.BlockSpec((None, seq, MLSTM_W), lambda b: (b, 0, j))
    whole = lambda *s: pl.BlockSpec((None,) + s, lambda b: (b,) + (0,) * len(s))
    return pl.pallas_call(
        _mlstm_kernel,
        grid=(nb,),
        in_specs=[qvo_blk(0), whole(MLSTM_W, seq), qvo_blk(1), qvo_blk(2), whole(N_GATES, nc, CHUNK),
                  whole(MLSTM_W, ctx_len), whole(ctx_len, MLSTM_W), whole(N_GATES, ncc, CHUNK),
                  pl.BlockSpec((1, MLSTM_W), lambda b: (0, 0))],
        out_specs=whole(seq, MLSTM_W),
        out_shape=jax.ShapeDtypeStruct((nb, seq, MLSTM_W), BF16),
        scratch_shapes=[pltpu.VMEM((seq, MLSTM_W), F32),
                        pltpu.VMEM((2 * N_HEADS, HEAD_DIM, 2 * HEAD_DIM), F32),
                        pltpu.VMEM((2 * N_HEADS, nc, N_ROWS, CHUNK), F32)],
        compiler_params=pltpu.CompilerParams(dimension_semantics=("arbitrary",),
                                             vmem_limit_bytes=VMEM_LIMIT),
        name="mlstm",
    )(qvo, kt, qvo, qvo, gt.reshape(nb, N_GATES, nc, CHUNK), ckt, cv,
      cgt.reshape(nb, N_GATES, ncc, CHUNK), mh)


def _out_ffn_kernel(x1_ref, conv_ref, mh_ref, mod_ref, g_ref, wo_ref, wup_ref, wdn_ref, out_ref):
    rows = _sub_rows(x1_ref.shape[0])
    ys = [_dot(conv_ref[r, :], wo_ref[0:CONV_W, :]) + _dot(mh_ref[r, :], wo_ref[CONV_W:, :]) for r in rows]
    x2s = [x1_ref[r, :] + mod_ref[5] * (_rms(y) * g_ref[3]) for r, y in zip(rows, ys)]
    hs = [_ffn_in(x2, g_ref[4], mod_ref[6], mod_ref[7]) for x2 in x2s]
    accs = [_swiglu(h, wup_ref, wdn_ref) for h in hs]
    for r, x2, acc in zip(rows, x2s, accs):
        out_ref[r, :] = _ffn_out(x2, acc, g_ref[5], mod_ref[8])


def _out_ffn(x1, conv, mh, mod4, g3, wo, wup, wdn, seq, tm):
    n, d = x1.shape
    tpb = seq // tm
    tile = lambda w: pl.BlockSpec((tm, w), lambda i: (i, 0))
    return pl.pallas_call(
        _out_ffn_kernel,
        grid=(n // tm,),
        in_specs=[tile(d), tile(CONV_W), tile(MLSTM_W),
                  pl.BlockSpec((None, N_MOD, 1, d), lambda i: (i // tpb, 0, 0, 0)),
                  _resident(g3.shape), _resident(wo.shape), _resident(wup.shape), _resident(wdn.shape)],
        out_specs=tile(d),
        out_shape=jax.ShapeDtypeStruct((n, d), F32),
        compiler_params=pltpu.CompilerParams(dimension_semantics=("arbitrary",),
                                             vmem_limit_bytes=VMEM_LIMIT),
        name="outproj_ffn2",
    )(x1, conv, mh, mod4, g3, wo, wup, wdn)


def kernel(x, c, ctx, c_ctx, w_mod, b_mod, norm_g, ffn1_up, ffn1_down, ffn2_up, ffn2_down,
           w_in, b_gates, conv_w, mh_norm, w_out):
    nb, seq, d = x.shape
    ctx_len = ctx.shape[1]
    depth = w_mod.shape[0]
    assert depth == 1, "only the single (last) layer configuration is implemented"
    assert nb + 1 <= MOD_ROWS and seq % GRID_W == 0
    tm = min(512, seq)
    conv_cols = 3 * CONV_W

    cvec = jnp.concatenate([c, c_ctx[None], jnp.zeros((MOD_ROWS - nb - 1, d), F32)], axis=0)
    mod4 = _modulation(cvec, w_mod[0], b_mod[0]).reshape(MOD_ROWS, N_MOD, 1, d)

    g3 = norm_g[0].reshape(6, 1, d)
    w1u, w1d = ffn1_up[0].astype(BF16), ffn1_down[0].astype(BF16)
    w2u, w2d = ffn2_up[0].astype(BF16), ffn2_down[0].astype(BF16)
    win = w_in[0]
    seg = lambda j: win[:, conv_cols + j * MLSTM_W:conv_cols + (j + 1) * MLSTM_W]
    wc = win[:, :conv_cols].astype(BF16)
    wqvo = jnp.concatenate([seg(0), seg(2), seg(3)], axis=1).astype(BF16)
    wv = seg(2).astype(BF16)
    wkt = seg(1).T.astype(BF16)
    wg = win[:, conv_cols + 4 * MLSTM_W:].T.astype(BF16)
    bg = b_gates[0].reshape(N_GATES, 1)
    cw3 = conv_w[0].reshape(3, 1, CONV_W)
    wo = w_out[0].astype(BF16)
    mh = mh_norm[0].reshape(1, MLSTM_W)

    x1, conv, qvo, kt, gt = _ffn_proj(x.reshape(nb * seq, d), mod4, g3, w1u, w1d, wc, wqvo, wkt, wg, bg,
                                      cw3, seq, tm)
    cv, ckt, cgt = _ffn_proj_ctx(ctx.reshape(nb * ctx_len, d), mod4, g3, w1u, w1d, wv, wkt, wg, bg,
                                 ctx_len, nb)
    hm = _mlstm(qvo.reshape(nb, seq, 3 * MLSTM_W), kt, gt, cv.reshape(nb, ctx_len, MLSTM_W), ckt, cgt, mh)
    out = _out_ffn(x1, conv, hm.reshape(nb * seq, MLSTM_W), mod4, g3, wo, w2u, w2d, seq, tm)
    return out.reshape(nb, seq, d)
```

```python
import jax
import jax.numpy as jnp
from jax import lax
from jax.experimental import pallas as pl
from jax.experimental.pallas import tpu as pltpu

F32 = jnp.float32
BF16 = jnp.bfloat16

GRID_W = 64
CONV_W = 512
N_HEADS = 4
HEAD_DIM = 128
MLSTM_W = N_HEADS * HEAD_DIM
CHUNK = 128
N_MOD = 9
EPS = 1e-6
LOG2E = 1.4426950408889634
N_GATES = 4 * N_HEADS
MOD_ROWS = 24
VMEM_LIMIT = 56 * 1024 * 1024

FFN_CHUNK = 512
SUB_ROWS = 256
CTX_PER_STEP = 2

ROW_A, ROW_M, ROW_E, ROW_W, ROW_DECAY, ROW_MIN, N_ROWS = 0, 1, 2, 3, 4, 5, 8


def _dot(a, b):
    return jnp.dot(a, b, preferred_element_type=F32)


def _dot_nt(a, b):
    return lax.dot_general(a, b, (((1,), (1,)), ((), ())), preferred_element_type=F32)


def _rms(x):
    return x * lax.rsqrt(jnp.mean(x * x, axis=-1, keepdims=True) + EPS)


def _resident(shape):
    nd = len(shape)
    return pl.BlockSpec(shape, lambda *_: (0,) * nd, pipeline_mode=pl.Buffered(1))


def _mod_kernel(c_ref, w_ref, b_ref, o_ref):
    cv = c_ref[...]
    s = (cv * jax.nn.sigmoid(cv)).astype(BF16)
    o_ref[...] = _dot(s, w_ref[...].astype(BF16)) + b_ref[...]


def _modulation(cvec, w_mod, b_mod):
    d = cvec.shape[1]
    n = w_mod.shape[1]
    bn = n // N_MOD
    return pl.pallas_call(
        _mod_kernel,
        grid=(N_MOD,),
        in_specs=[pl.BlockSpec((MOD_ROWS, d), lambda j: (0, 0)),
                  pl.BlockSpec((d, bn), lambda j: (0, j)),
                  pl.BlockSpec((1, bn), lambda j: (0, j))],
        out_specs=pl.BlockSpec((MOD_ROWS, bn), lambda j: (0, j)),
        out_shape=jax.ShapeDtypeStruct((MOD_ROWS, n), F32),
        compiler_params=pltpu.CompilerParams(dimension_semantics=("arbitrary",),
                                             vmem_limit_bytes=VMEM_LIMIT),
        name="modulation",
    )(cvec, w_mod, b_mod.reshape(1, n))


def _ffn_in(x, g_pre, shift, scale):
    return (_rms(x) * g_pre * (1.0 + scale) + shift).astype(BF16)


def _swiglu(h, wup_ref, wdn_ref):
    ffn = wdn_ref.shape[0]
    chunks = [(c0, min(FFN_CHUNK, ffn - c0)) for c0 in range(0, ffn, FFN_CHUNK)]

    def up(c0, cw):
        return _dot(h, wup_ref[:, c0:c0 + cw]), _dot(h, wup_ref[:, ffn + c0:ffn + c0 + cw])

    acc = None
    ab = up(*chunks[0])
    for i, (c0, cw) in enumerate(chunks):
        a, b = ab
        if i + 1 < len(chunks):
            ab = up(*chunks[i + 1])
        act = (a * jax.nn.sigmoid(a) * b).astype(BF16)
        part = _dot(act, wdn_ref[c0:c0 + cw, :])
        acc = part if acc is None else acc + part
    return acc


def _ffn_out(x, acc, g_post, gate):
    return x + 0.5 * gate * (_rms(acc) * g_post)


def _sub_rows(n_rows):
    return [pl.ds(r, SUB_ROWS) for r in range(0, n_rows, SUB_ROWS)]


def _log_sigmoid(x):
    return jnp.minimum(x, 0.0) - jnp.log1p(jnp.exp(-jnp.abs(x)))


def _gates_t(h, wg_ref, bg_ref):
    g = _dot_nt(wg_ref[...], h) + bg_ref[...]
    half = N_GATES // 2
    return jnp.concatenate([g[:half], _log_sigmoid(g[half:])], axis=0)


def _mixer_in(x1, g_ref, mod_ref):
    return (_rms(x1) * g_ref[2] * (1.0 + mod_ref[4]) + mod_ref[3]).astype(BF16)


def _ffn_proj_kernel(x_ref, mod_ref, g_ref, wup_ref, wdn_ref, wc_ref, wqvo_ref, wkt_ref, wg_ref, bg_ref,
                     cw_ref, x1_ref, conv_ref, qvo_ref, kt_ref, gt_ref):
    rows = _sub_rows(x_ref.shape[0])
    hs = [_ffn_in(x_ref[r, :], g_ref[0], mod_ref[0], mod_ref[1]) for r in rows]
    accs = [_swiglu(h, wup_ref, wdn_ref) for h in hs]
    h2s = []
    for r, acc in zip(rows, accs):
        x1 = _ffn_out(x_ref[r, :], acc, g_ref[1], mod_ref[2])
        x1_ref[r, :] = x1
        h2s.append(_mixer_in(x1, g_ref, mod_ref))

    for r, h in zip(rows, h2s):
        sub = h.shape[0]
        bg = _dot(h, wc_ref[:, 0:CONV_W])
        cu = _dot(h, wc_ref[:, CONV_W:2 * CONV_W]) * _dot(h, wc_ref[:, 2 * CONV_W:3 * CONV_W])
        col = lax.broadcasted_iota(jnp.int32, (sub, 1), 0) % GRID_W
        prev = jnp.where(col == 0, 0.0, pltpu.roll(cu, 1, 0))
        nxt = jnp.where(col == GRID_W - 1, 0.0, pltpu.roll(cu, sub - 1, 0))
        conv = bg * (cw_ref[0] * prev + cw_ref[1] * cu + cw_ref[2] * nxt)
        conv_ref[r, :] = conv.astype(BF16)

        for j in range(3):
            cols = slice(j * MLSTM_W, (j + 1) * MLSTM_W)
            qvo_ref[r, cols] = _dot(h, wqvo_ref[:, cols]).astype(BF16)
        kt_ref[:, r] = (_dot_nt(wkt_ref[...], h) * (HEAD_DIM ** -0.5)).astype(BF16)
        gt_ref[:, r] = _gates_t(h, wg_ref, bg_ref)


def _ffn_proj(x2d, mod4, g3, wup, wdn, wc, wqvo, wkt, wg, bg, cw3, seq, tm):
    n, d = x2d.shape
    tpb = seq // tm
    nb = n // seq
    tile = lambda w: pl.BlockSpec((tm, w), lambda i: (i, 0))
    tile_t = lambda r: pl.BlockSpec((None, r, tm), lambda i: (i // tpb, 0, i % tpb))
    return pl.pallas_call(
        _ffn_proj_kernel,
        grid=(n // tm,),
        in_specs=[tile(d),
                  pl.BlockSpec((None, N_MOD, 1, d), lambda i: (i // tpb, 0, 0, 0)),
                  _resident(g3.shape), _resident(wup.shape), _resident(wdn.shape),
                  _resident(wc.shape), _resident(wqvo.shape), _resident(wkt.shape), _resident(wg.shape),
                  _resident(bg.shape), _resident(cw3.shape)],
        out_specs=[tile(d), tile(CONV_W), tile(3 * MLSTM_W), tile_t(MLSTM_W), tile_t(N_GATES)],
        out_shape=[jax.ShapeDtypeStruct((n, d), F32),
                   jax.ShapeDtypeStruct((n, CONV_W), BF16),
                   jax.ShapeDtypeStruct((n, 3 * MLSTM_W), BF16),
                   jax.ShapeDtypeStruct((nb, MLSTM_W, seq), BF16),
                   jax.ShapeDtypeStruct((nb, N_GATES, seq), F32)],
        compiler_params=pltpu.CompilerParams(dimension_semantics=("arbitrary",),
                                             vmem_limit_bytes=VMEM_LIMIT),
        name="ffn1_inproj",
    )(x2d, mod4, g3, wup, wdn, wc, wqvo, wkt, wg, bg, cw3)


def _ffn_proj_ctx_kernel(x_ref, mod_ref, g_ref, wup_ref, wdn_ref, wv_ref, wkt_ref, wg_ref, bg_ref,
                         v_ref, kt_ref, gt_ref):
    n_sub, ctx_len = kt_ref.shape[0], kt_ref.shape[2]
    rows = [pl.ds(r * ctx_len, ctx_len) for r in range(n_sub)]
    hs = [_ffn_in(x_ref[r, :], g_ref[0], mod_ref[0], mod_ref[1]) for r in rows]
    accs = [_swiglu(h, wup_ref, wdn_ref) for h in hs]
    h2s = [_mixer_in(_ffn_out(x_ref[r, :], acc, g_ref[1], mod_ref[2]), g_ref, mod_ref)
           for r, acc in zip(rows, accs)]
    for j, (r, h) in enumerate(zip(rows, h2s)):
        v_ref[r, :] = _dot(h, wv_ref[...]).astype(BF16)
        kt_ref[j] = (_dot_nt(wkt_ref[...], h) * (HEAD_DIM ** -0.5)).astype(BF16)
        gt_ref[j] = _gates_t(h, wg_ref, bg_ref)


def _ffn_proj_ctx(c2d, mod4, g3, wup, wdn, wv, wkt, wg, bg, ctx_len, ctx_row):
    n, d = c2d.shape
    assert (n // ctx_len) % CTX_PER_STEP == 0
    tm = CTX_PER_STEP * ctx_len
    tile_t = lambda r: pl.BlockSpec((CTX_PER_STEP, r, ctx_len), lambda i: (i, 0, 0))
    return pl.pallas_call(
        _ffn_proj_ctx_kernel,
        grid=(n // tm,),
        in_specs=[pl.BlockSpec((tm, d), lambda i: (i, 0)),
                  pl.BlockSpec((None, N_MOD, 1, d), lambda i: (ctx_row, 0, 0, 0)),
                  _resident(g3.shape), _resident(wup.shape), _resident(wdn.shape),
                  _resident(wv.shape), _resident(wkt.shape), _resident(wg.shape), _resident(bg.shape)],
        out_specs=[pl.BlockSpec((tm, MLSTM_W), lambda i: (i, 0)), tile_t(MLSTM_W), tile_t(N_GATES)],
        out_shape=[jax.ShapeDtypeStruct((n, MLSTM_W), BF16),
                   jax.ShapeDtypeStruct((n // ctx_len, MLSTM_W, ctx_len), BF16),
                   jax.ShapeDtypeStruct((n // ctx_len, N_GATES, ctx_len), F32)],
        compiler_params=pltpu.CompilerParams(dimension_semantics=("arbitrary",),
                                             vmem_limit_bytes=VMEM_LIMIT),
        name="ffn1_inproj_ctx",
    )(c2d, mod4, g3, wup, wdn, wv, wkt, wg, bg)


def _lane_scan(x, op, reverse):
    pos = lax.broadcasted_iota(jnp.int32, x.shape, 1)
    d = 1
    while d < CHUNK:
        if reverse:
            x = jnp.where(pos < CHUNK - d, op(x, pltpu.roll(x, CHUNK - d, 1)), x)
        else:
            x = jnp.where(pos >= d, op(x, pltpu.roll(x, d, 1)), x)
        d *= 2
    return x


def _lane_pick(x, lane):
    idx = lax.broadcasted_iota(jnp.int32, x.shape, 1)
    return jnp.sum(jnp.where(idx == lane, x, 0.0), axis=1, keepdims=True)


def _chunk_rows(c):
    if isinstance(c, int):
        return pl.ds(c * CHUNK, CHUNK)
    return pl.ds(pl.multiple_of(c * CHUNK, CHUNK), CHUNK)


def _gate_scan(ig, lf, m, reverse):
    n = ig.shape[0]
    b = _lane_scan(lf, jnp.add, reverse)
    a = ig - b
    b_last = _lane_pick(b, 0 if reverse else CHUNK - 1)
    wmax = b_last + jnp.max(a, axis=1, keepdims=True)
    sub = lax.broadcasted_iota(jnp.int32, (n, 1), 0)
    m_in = jnp.zeros((n, 1), F32)
    m_out = jnp.zeros((n, 1), F32)
    for c in (reversed(range(n)) if reverse else range(n)):
        m_new = jnp.maximum(b_last[c:c + 1] + m, wmax[c:c + 1])
        m_in = jnp.where(sub == c, m, m_in)
        m_out = jnp.where(sub == c, m_new, m_out)
        m = m_new
    big_m = jnp.maximum(m_in, _lane_scan(a, jnp.maximum, reverse))
    e = jnp.exp(-(b + big_m))
    w = jnp.exp(a + b_last - m_out)
    decay = jnp.exp(b_last + m_in - m_out)
    return (a, big_m, e, w, decay, m_in), m


def _v_aug(v):
    return jnp.concatenate([v, jnp.ones((CHUNK, HEAD_DIM), BF16)], axis=1)


def _mlstm_kernel(q_ref, kt_ref, v_ref, g_ref, ckt_ref, cv_ref, cg_ref, out_ref, s_ref, rows_ref, neg_ref):
    nc = kt_ref.shape[1] // CHUNK
    ncc = ckt_ref.shape[1] // CHUNK

    for h in range(N_HEADS):
        hs = slice(h * HEAD_DIM, (h + 1) * HEAD_DIM)
        for d in range(2):
            hd = 2 * h + d
            rev = d == 1
            i_row, f_row = d * N_HEADS + h, (2 + d) * N_HEADS + h
            (_, _, _, w, decay, _), m = _gate_scan(cg_ref[i_row], cg_ref[f_row], jnp.zeros((1, 1), F32), rev)
            s = jnp.zeros((HEAD_DIM, 2 * HEAD_DIM), F32)
            for c in (reversed(range(ncc)) if rev else range(ncc)):
                tok = _chunk_rows(c)
                kw = (ckt_ref[hs, tok].astype(F32) * w[c:c + 1]).astype(BF16)
                s = decay[c:c + 1] * s + _dot(kw, _v_aug(cv_ref[tok, hs]))
            s_ref[hd] = s
            (a, big_m, e, w, decay, m_in), _ = _gate_scan(g_ref[i_row], g_ref[f_row], m, rev)
            decay = jnp.broadcast_to(decay, (nc, CHUNK))
            m_in = jnp.broadcast_to(m_in, (nc, CHUNK))
            vals = ((ROW_A, a * LOG2E), (ROW_M, big_m * LOG2E), (ROW_E, e), (ROW_W, w),
                    (ROW_DECAY, decay), (ROW_MIN, m_in * LOG2E))
            for c in range(nc):
                for r, val in vals:
                    rows_ref[hd, c, r:r + 1, :] = val[c:c + 1]

    row_id = lax.broadcasted_iota(jnp.int32, (CHUNK, CHUNK), 0)
    col_id = lax.broadcasted_iota(jnp.int32, (CHUNK, CHUNK), 1)
    neg_ref[0] = jnp.where(col_id <= row_id, 0.0, -jnp.inf)
    neg_ref[1] = jnp.where(col_id >= row_id, 0.0, -jnp.inf)

    def to_col(row):
        return jnp.sum(jnp.where(row_id == col_id, row, 0.0), axis=1, keepdims=True)

    def scores(h, d, c):
        hs = slice(h * HEAD_DIM, (h + 1) * HEAD_DIM)
        tok = _chunk_rows(c)
        rows = rows_ref[2 * h + d, c]
        m_col = to_col(rows[ROW_M:ROW_M + 1])
        g_col = jnp.exp2(rows[ROW_MIN:ROW_MIN + 1] - m_col)
        dmat = jnp.exp2(rows[ROW_A:ROW_A + 1] - m_col + neg_ref[d])
        q = q_ref[tok, hs]
        kt = kt_ref[hs, tok]
        lhs = jnp.concatenate([(_dot(q, kt) * dmat).astype(BF16),
                               (q.astype(F32) * g_col).astype(BF16)], axis=1)
        kw = (kt.astype(F32) * rows[ROW_W:ROW_W + 1]).astype(BF16)
        return lhs, kw, rows

    def readout(h, d, c, lhs, kw, rows):
        hs = slice(h * HEAD_DIM, (h + 1) * HEAD_DIM)
        v_aug = _v_aug(v_ref[_chunk_rows(c), hs])
        s_old = s_ref[2 * h + d]
        tot = _dot(lhs, jnp.concatenate([v_aug, s_old.astype(BF16)], axis=0))
        dec = rows[ROW_DECAY:ROW_DECAY + 1]
        s_ref[2 * h + d] = jnp.concatenate([dec, dec], axis=1) * s_old + _dot(kw, v_aug)
        return tot

    def normalise(tot, rows):
        e_col = to_col(rows[ROW_E:ROW_E + 1])
        return tot[:, :HEAD_DIM] / jnp.maximum(jnp.abs(tot[:, HEAD_DIM:]), e_col)

    def iteration(i, accumulate):
        scans = [(h, d, (nc - 1 - i) if d else i) for h in range(N_HEADS) for d in range(2)]
        staged = [scores(*s) for s in scans]
        tots = [readout(*s, *st) for s, st in zip(scans, staged)]
        for (h, d, c), tot, st in zip(scans, tots, staged):
            hh = normalise(tot, st[2])
            dst = (_chunk_rows(c), slice(h * HEAD_DIM, (h + 1) * HEAD_DIM))
            out_ref[dst] = out_ref[dst] + hh if accumulate else hh

    def first_half(i, carry):
        iteration(i, False)
        return carry

    def second_half(i, carry):
        iteration(i, True)
        return carry

    lax.fori_loop(0, nc // 2, first_half, 0, unroll=2)
    lax.fori_loop(nc // 2, nc, second_half, 0, unroll=2)


def _mlstm(qvo, kt, gt, cv, ckt, cgt):
    nb, seq, _ = qvo.shape
    ctx_len = cv.shape[1]
    assert seq % (2 * CHUNK) == 0 and ctx_len % CHUNK == 0
    nc, ncc = seq // CHUNK, ctx_len // CHUNK
    qvo_blk = lambda j: pl.BlockSpec((None, seq, MLSTM_W), lambda b: (b, 0, j))
    whole = lambda *s: pl.BlockSpec((None,) + s, lambda b: (b,) + (0,) * len(s))
    return pl.pallas_call(
        _mlstm_kernel,
        grid=(nb,),
        in_specs=[qvo_blk(0), whole(MLSTM_W, seq), qvo_blk(1), whole(N_GATES, nc, CHUNK),
                  whole(MLSTM_W, ctx_len), whole(ctx_len, MLSTM_W), whole(N_GATES, ncc, CHUNK)],
        out_specs=whole(seq, MLSTM_W),
        out_shape=jax.ShapeDtypeStruct((nb, seq, MLSTM_W), F32),
        scratch_shapes=[pltpu.VMEM((2 * N_HEADS, HEAD_DIM, 2 * HEAD_DIM), F32),
                        pltpu.VMEM((2 * N_HEADS, nc, N_ROWS, CHUNK), F32),
                        pltpu.VMEM((2, CHUNK, CHUNK), F32)],
        compiler_params=pltpu.CompilerParams(dimension_semantics=("arbitrary",),
                                             vmem_limit_bytes=VMEM_LIMIT),
        name="mlstm",
    )(qvo, kt, qvo, gt.reshape(nb, N_GATES, nc, CHUNK), ckt, cv, cgt.reshape(nb, N_GATES, ncc, CHUNK))


def _mlstm_out(h, o, mh_norm):
    hn = jnp.concatenate([_rms(h[:, j * HEAD_DIM:(j + 1) * HEAD_DIM]) for j in range(N_HEADS)], axis=1)
    return (jax.nn.sigmoid(o.astype(F32)) * (hn * mh_norm)).astype(BF16)


def _out_ffn_kernel(x1_ref, conv_ref, h_ref, o_ref, mod_ref, g_ref, mhn_ref, wo_ref, wup_ref, wdn_ref, out_ref):
    rows = _sub_rows(x1_ref.shape[0])
    hms = [_mlstm_out(h_ref[r, :], o_ref[r, :], mhn_ref[...]) for r in rows]
    ys = [_dot(conv_ref[r, :], wo_ref[0:CONV_W, :]) + _dot(hm, wo_ref[CONV_W:, :]) for r, hm in zip(rows, hms)]
    x2s = [x1_ref[r, :] + mod_ref[5] * (_rms(y) * g_ref[3]) for r, y in zip(rows, ys)]
    hs = [_ffn_in(x2, g_ref[4], mod_ref[6], mod_ref[7]) for x2 in x2s]
    accs = [_swiglu(h, wup_ref, wdn_ref) for h in hs]
    for r, x2, acc in zip(rows, x2s, accs):
        out_ref[r, :] = _ffn_out(x2, acc, g_ref[5], mod_ref[8])


def _out_ffn(x1, conv, hsum, qvo, mod4, g3, mhn, wo, wup, wdn, seq, tm):
    n, d = x1.shape
    tpb = seq // tm
    tile = lambda w: pl.BlockSpec((tm, w), lambda i: (i, 0))
    return pl.pallas_call(
        _out_ffn_kernel,
        grid=(n // tm,),
        in_specs=[tile(d), tile(CONV_W), tile(MLSTM_W),
                  pl.BlockSpec((tm, MLSTM_W), lambda i: (i, 2)),
                  pl.BlockSpec((None, N_MOD, 1, d), lambda i: (i // tpb, 0, 0, 0)),
                  _resident(g3.shape), _resident(mhn.shape), _resident(wo.shape), _resident(wup.shape),
                  _resident(wdn.shape)],
        out_specs=tile(d),
        out_shape=jax.ShapeDtypeStruct((n, d), F32),
        compiler_params=pltpu.CompilerParams(dimension_semantics=("arbitrary",),
                                             vmem_limit_bytes=VMEM_LIMIT),
        name="outproj_ffn2",
    )(x1, conv, hsum, qvo, mod4, g3, mhn, wo, wup, wdn)


def kernel(x, c, ctx, c_ctx, w_mod, b_mod, norm_g, ffn1_up, ffn1_down, ffn2_up, ffn2_down,
           w_in, b_gates, conv_w, mh_norm, w_out):
    nb, seq, d = x.shape
    ctx_len = ctx.shape[1]
    depth = w_mod.shape[0]
    assert depth == 1, "only the single (last) layer configuration is implemented"
    assert nb + 1 <= MOD_ROWS and seq % GRID_W == 0
    tm_in, tm_out = min(512, seq), min(1024, seq)
    assert tm_in % SUB_ROWS == 0 and tm_out % SUB_ROWS == 0 and SUB_ROWS % GRID_W == 0
    conv_cols = 3 * CONV_W

    cvec = jnp.concatenate([c, c_ctx[None], jnp.zeros((MOD_ROWS - nb - 1, d), F32)], axis=0)
    mod4 = _modulation(cvec, w_mod[0], b_mod[0]).reshape(MOD_ROWS, N_MOD, 1, d)

    g3 = norm_g[0].reshape(6, 1, d)
    w1u, w1d = ffn1_up[0].astype(BF16), ffn1_down[0].astype(BF16)
    w2u, w2d = ffn2_up[0].astype(BF16), ffn2_down[0].astype(BF16)
    win = w_in[0]
    seg = lambda j: win[:, conv_cols + j * MLSTM_W:conv_cols + (j + 1) * MLSTM_W]
    wc = win[:, :conv_cols].astype(BF16)
    wqvo = jnp.concatenate([seg(0), seg(2), seg(3)], axis=1).astype(BF16)
    wv = seg(2).astype(BF16)
    wkt = seg(1).T.astype(BF16)
    wg = win[:, conv_cols + 4 * MLSTM_W:].T.astype(BF16)
    bg = b_gates[0].reshape(N_GATES, 1)
    cw3 = conv_w[0].reshape(3, 1, CONV_W)
    wo = w_out[0].astype(BF16)
    mh = mh_norm[0].reshape(1, MLSTM_W)

    x1, conv, qvo, kt, gt = _ffn_proj(x.reshape(nb * seq, d), mod4, g3, w1u, w1d, wc, wqvo, wkt, wg, bg,
                                      cw3, seq, tm_in)
    cv, ckt, cgt = _ffn_proj_ctx(ctx.reshape(nb * ctx_len, d), mod4, g3, w1u, w1d, wv, wkt, wg, bg,
                                 ctx_len, nb)
    hsum = _mlstm(qvo.reshape(nb, seq, 3 * MLSTM_W), kt, gt, cv.reshape(nb, ctx_len, MLSTM_W), ckt, cgt)
    out = _out_ffn(x1, conv, hsum.reshape(nb * seq, MLSTM_W), qvo, mod4, g3, mh, wo, w2u, w2d, seq, tm_out)
    return out.reshape(nb, seq, d)
```

```python
import jax
import jax.numpy as jnp
from jax import lax
from jax.experimental import pallas as pl
from jax.experimental.pallas import tpu as pltpu

F32 = jnp.float32
BF16 = jnp.bfloat16

GRID_W = 64
CONV_W = 512
N_HEADS = 4
HEAD_DIM = 128
MLSTM_W = N_HEADS * HEAD_DIM
CHUNK = 128
N_MOD = 9
EPS = 1e-6
LOG2E = 1.4426950408889634
N_GATES = 4 * N_HEADS
MOD_ROWS = 24
VMEM_LIMIT = 62 * 1024 * 1024

FFN_CHUNK = 512
SUB_ROWS = 256
CTX_PER_STEP = 2

ROW_A, ROW_M, ROW_E, ROW_W, ROW_DECAY, ROW_MIN, N_ROWS = 0, 1, 2, 3, 4, 5, 8


def _dot(a, b):
    return jnp.dot(a, b, preferred_element_type=F32)


def _dot_nt(a, b):
    return lax.dot_general(a, b, (((1,), (1,)), ((), ())), preferred_element_type=F32)


def _rms(x):
    return x * lax.rsqrt(jnp.mean(x * x, axis=-1, keepdims=True) + EPS)


def _resident(shape):
    nd = len(shape)
    return pl.BlockSpec(shape, lambda *_: (0,) * nd, pipeline_mode=pl.Buffered(1))


def _mod_kernel(c_ref, w_ref, b_ref, o_ref):
    cv = c_ref[...]
    s = (cv * jax.nn.sigmoid(cv)).astype(BF16)
    o_ref[...] = _dot(s, w_ref[...].astype(BF16)) + b_ref[...]


def _modulation(cvec, w_mod, b_mod):
    d = cvec.shape[1]
    n = w_mod.shape[1]
    bn = n // N_MOD
    return pl.pallas_call(
        _mod_kernel,
        grid=(N_MOD,),
        in_specs=[pl.BlockSpec((MOD_ROWS, d), lambda j: (0, 0)),
                  pl.BlockSpec((d, bn), lambda j: (0, j)),
                  pl.BlockSpec((1, bn), lambda j: (0, j))],
        out_specs=pl.BlockSpec((MOD_ROWS, bn), lambda j: (0, j)),
        out_shape=jax.ShapeDtypeStruct((MOD_ROWS, n), F32),
        compiler_params=pltpu.CompilerParams(dimension_semantics=("arbitrary",),
                                             vmem_limit_bytes=VMEM_LIMIT),
        name="modulation",
    )(cvec, w_mod, b_mod.reshape(1, n))


def _ffn_in(x, g_pre, shift, scale):
    return (_rms(x) * g_pre * (1.0 + scale) + shift).astype(BF16)


def _swiglu(h, wup_ref, wdn_ref):
    ffn = wdn_ref.shape[0]
    chunks = [(c0, min(FFN_CHUNK, ffn - c0)) for c0 in range(0, ffn, FFN_CHUNK)]

    def up(c0, cw):
        return _dot(h, wup_ref[:, c0:c0 + cw]), _dot(h, wup_ref[:, ffn + c0:ffn + c0 + cw])

    acc = None
    ab = up(*chunks[0])
    for i, (c0, cw) in enumerate(chunks):
        a, b = ab
        if i + 1 < len(chunks):
            ab = up(*chunks[i + 1])
        act = (a * jax.nn.sigmoid(a) * b).astype(BF16)
        part = _dot(act, wdn_ref[c0:c0 + cw, :])
        acc = part if acc is None else acc + part
    return acc


def _ffn_out(x, acc, g_post, gate):
    return x + 0.5 * gate * (_rms(acc) * g_post)


def _sub_rows(n_rows):
    return [pl.ds(r, SUB_ROWS) for r in range(0, n_rows, SUB_ROWS)]


def _log_sigmoid(x):
    return jnp.minimum(x, 0.0) - jnp.log1p(jnp.exp(-jnp.abs(x)))


def _gates_t(h, wg_ref, bg_ref):
    g = _dot_nt(wg_ref[...], h) + bg_ref[...]
    half = N_GATES // 2
    return jnp.concatenate([g[:half], _log_sigmoid(g[half:])], axis=0)


def _mixer_in(x1, g_ref, mod_ref):
    return (_rms(x1) * g_ref[2] * (1.0 + mod_ref[4]) + mod_ref[3]).astype(BF16)


def _ffn_proj_kernel(x_ref, mod_ref, g_ref, wup_ref, wdn_ref, wc_ref, wqvo_ref, wkt_ref, wg_ref, bg_ref,
                     cw_ref, x1_ref, conv_ref, qvo_ref, kt_ref, gt_ref):
    rows = _sub_rows(x_ref.shape[0])
    hs = [_ffn_in(x_ref[r, :], g_ref[0], mod_ref[0], mod_ref[1]) for r in rows]
    accs = [_swiglu(h, wup_ref, wdn_ref) for h in hs]
    h2s = []
    for r, acc in zip(rows, accs):
        x1 = _ffn_out(x_ref[r, :], acc, g_ref[1], mod_ref[2])
        x1_ref[r, :] = x1
        h2s.append(_mixer_in(x1, g_ref, mod_ref))

    for r, h in zip(rows, h2s):
        sub = h.shape[0]
        bg = _dot(h, wc_ref[:, 0:CONV_W])
        cu = _dot(h, wc_ref[:, CONV_W:2 * CONV_W]) * _dot(h, wc_ref[:, 2 * CONV_W:3 * CONV_W])
        col = lax.broadcasted_iota(jnp.int32, (sub, 1), 0) % GRID_W
        prev = jnp.where(col == 0, 0.0, pltpu.roll(cu, 1, 0))
        nxt = jnp.where(col == GRID_W - 1, 0.0, pltpu.roll(cu, sub - 1, 0))
        conv = bg * (cw_ref[0] * prev + cw_ref[1] * cu + cw_ref[2] * nxt)
        conv_ref[r, :] = conv.astype(BF16)

        for j in range(3):
            cols = slice(j * MLSTM_W, (j + 1) * MLSTM_W)
            qvo_ref[r, cols] = _dot(h, wqvo_ref[:, cols]).astype(BF16)
        kt_ref[:, r] = (_dot_nt(wkt_ref[...], h) * (HEAD_DIM ** -0.5)).astype(BF16)
        gt_ref[:, r] = _gates_t(h, wg_ref, bg_ref)


def _ffn_proj(x2d, mod4, g3, wup, wdn, wc, wqvo, wkt, wg, bg, cw3, seq, tm):
    n, d = x2d.shape
    tpb = seq // tm
    nb = n // seq
    tile = lambda w: pl.BlockSpec((tm, w), lambda i: (i, 0))
    tile_t = lambda r: pl.BlockSpec((None, r, tm), lambda i: (i // tpb, 0, i % tpb))
    return pl.pallas_call(
        _ffn_proj_kernel,
        grid=(n // tm,),
        in_specs=[tile(d),
                  pl.BlockSpec((None, N_MOD, 1, d), lambda i: (i // tpb, 0, 0, 0)),
                  _resident(g3.shape), _resident(wup.shape), _resident(wdn.shape),
                  _resident(wc.shape), _resident(wqvo.shape), _resident(wkt.shape), _resident(wg.shape),
                  _resident(bg.shape), _resident(cw3.shape)],
        out_specs=[tile(d), tile(CONV_W), tile(3 * MLSTM_W), tile_t(MLSTM_W), tile_t(N_GATES)],
        out_shape=[jax.ShapeDtypeStruct((n, d), F32),
                   jax.ShapeDtypeStruct((n, CONV_W), BF16),
                   jax.ShapeDtypeStruct((n, 3 * MLSTM_W), BF16),
                   jax.ShapeDtypeStruct((nb, MLSTM_W, seq), BF16),
                   jax.ShapeDtypeStruct((nb, N_GATES, seq), F32)],
        compiler_params=pltpu.CompilerParams(dimension_semantics=("arbitrary",),
                                             vmem_limit_bytes=VMEM_LIMIT),
        name="ffn1_inproj",
    )(x2d, mod4, g3, wup, wdn, wc, wqvo, wkt, wg, bg, cw3)


def _ffn_proj_ctx_kernel(x_ref, mod_ref, g_ref, wup_ref, wdn_ref, wv_ref, wkt_ref, wg_ref, bg_ref,
                         v_ref, kt_ref, gt_ref):
    n_sub, ctx_len = kt_ref.shape[0], kt_ref.shape[2]
    rows = [pl.ds(r * ctx_len, ctx_len) for r in range(n_sub)]
    hs = [_ffn_in(x_ref[r, :], g_ref[0], mod_ref[0], mod_ref[1]) for r in rows]
    accs = [_swiglu(h, wup_ref, wdn_ref) for h in hs]
    h2s = [_mixer_in(_ffn_out(x_ref[r, :], acc, g_ref[1], mod_ref[2]), g_ref, mod_ref)
           for r, acc in zip(rows, accs)]
    for j, (r, h) in enumerate(zip(rows, h2s)):
        v_ref[r, :] = _dot(h, wv_ref[...]).astype(BF16)
        kt_ref[j] = (_dot_nt(wkt_ref[...], h) * (HEAD_DIM ** -0.5)).astype(BF16)
        gt_ref[j] = _gates_t(h, wg_ref, bg_ref)


def _ffn_proj_ctx(c2d, mod4, g3, wup, wdn, wv, wkt, wg, bg, ctx_len, ctx_row):
    n, d = c2d.shape
    assert (n // ctx_len) % CTX_PER_STEP == 0
    tm = CTX_PER_STEP * ctx_len
    tile_t = lambda r: pl.BlockSpec((CTX_PER_STEP, r, ctx_len), lambda i: (i, 0, 0))
    return pl.pallas_call(
        _ffn_proj_ctx_kernel,
        grid=(n // tm,),
        in_specs=[pl.BlockSpec((tm, d), lambda i: (i, 0)),
                  pl.BlockSpec((None, N_MOD, 1, d), lambda i: (ctx_row, 0, 0, 0)),
                  _resident(g3.shape), _resident(wup.shape), _resident(wdn.shape),
                  _resident(wv.shape), _resident(wkt.shape), _resident(wg.shape), _resident(bg.shape)],
        out_specs=[pl.BlockSpec((tm, MLSTM_W), lambda i: (i, 0)), tile_t(MLSTM_W), tile_t(N_GATES)],
        out_shape=[jax.ShapeDtypeStruct((n, MLSTM_W), BF16),
                   jax.ShapeDtypeStruct((n // ctx_len, MLSTM_W, ctx_len), BF16),
                   jax.ShapeDtypeStruct((n // ctx_len, N_GATES, ctx_len), F32)],
        compiler_params=pltpu.CompilerParams(dimension_semantics=("arbitrary",),
                                             vmem_limit_bytes=VMEM_LIMIT),
        name="ffn1_inproj_ctx",
    )(c2d, mod4, g3, wup, wdn, wv, wkt, wg, bg)


def _lane_scan(x, op, reverse):
    pos = lax.broadcasted_iota(jnp.int32, x.shape, 1)
    d = 1
    while d < CHUNK:
        if reverse:
            x = jnp.where(pos < CHUNK - d, op(x, pltpu.roll(x, CHUNK - d, 1)), x)
        else:
            x = jnp.where(pos >= d, op(x, pltpu.roll(x, d, 1)), x)
        d *= 2
    return x


def _lane_pick(x, lane):
    idx = lax.broadcasted_iota(jnp.int32, x.shape, 1)
    return jnp.sum(jnp.where(idx == lane, x, 0.0), axis=1, keepdims=True)


def _chunk_rows(c):
    if isinstance(c, int):
        return pl.ds(c * CHUNK, CHUNK)
    return pl.ds(pl.multiple_of(c * CHUNK, CHUNK), CHUNK)


def _gate_scan(ig, lf, m, reverse):
    n = ig.shape[0]
    b = _lane_scan(lf, jnp.add, reverse)
    a = ig - b
    b_last = _lane_pick(b, 0 if reverse else CHUNK - 1)
    wmax = b_last + jnp.max(a, axis=1, keepdims=True)
    sub = lax.broadcasted_iota(jnp.int32, (n, 1), 0)
    m_in = jnp.zeros((n, 1), F32)
    m_out = jnp.zeros((n, 1), F32)
    for c in (reversed(range(n)) if reverse else range(n)):
        m_new = jnp.maximum(b_last[c:c + 1] + m, wmax[c:c + 1])
        m_in = jnp.where(sub == c, m, m_in)
        m_out = jnp.where(sub == c, m_new, m_out)
        m = m_new
    big_m = jnp.maximum(m_in, _lane_scan(a, jnp.maximum, reverse))
    e = jnp.exp(-(b + big_m))
    w = jnp.exp(a + b_last - m_out)
    decay = jnp.exp(b_last + m_in - m_out)
    return (a, big_m, e, w, decay, m_in), m


def _v_aug(v):
    return jnp.concatenate([v, jnp.ones((CHUNK, HEAD_DIM), BF16)], axis=1)


def _mlstm_kernel(q_ref, kt_ref, v_ref, g_ref, ckt_ref, cv_ref, cg_ref, out_ref, s_ref, rows_ref, neg_ref):
    nc = kt_ref.shape[1] // CHUNK
    ncc = ckt_ref.shape[1] // CHUNK

    for h in range(N_HEADS):
        hs = slice(h * HEAD_DIM, (h + 1) * HEAD_DIM)
        for d in range(2):
            hd = 2 * h + d
            rev = d == 1
            i_row, f_row = d * N_HEADS + h, (2 + d) * N_HEADS + h
            (_, _, _, w, decay, _), m = _gate_scan(cg_ref[i_row], cg_ref[f_row], jnp.zeros((1, 1), F32), rev)
            s = jnp.zeros((HEAD_DIM, 2 * HEAD_DIM), F32)
            for c in (reversed(range(ncc)) if rev else range(ncc)):
                tok = _chunk_rows(c)
                kw = (ckt_ref[hs, tok].astype(F32) * w[c:c + 1]).astype(BF16)
                s = decay[c:c + 1] * s + _dot(kw, _v_aug(cv_ref[tok, hs]))
            s_ref[hd] = s
            (a, big_m, e, w, decay, m_in), _ = _gate_scan(g_ref[i_row], g_ref[f_row], m, rev)
            decay = jnp.broadcast_to(decay, (nc, CHUNK))
            m_in = jnp.broadcast_to(m_in, (nc, CHUNK))
            vals = ((ROW_A, a * LOG2E), (ROW_M, big_m * LOG2E), (ROW_E, e), (ROW_W, w),
                    (ROW_DECAY, decay), (ROW_MIN, m_in * LOG2E))
            for c in range(nc):
                for r, val in vals:
                    rows_ref[hd, c, r:r + 1, :] = val[c:c + 1]

    row_id = lax.broadcasted_iota(jnp.int32, (CHUNK, CHUNK), 0)
    col_id = lax.broadcasted_iota(jnp.int32, (CHUNK, CHUNK), 1)
    neg_ref[0] = jnp.where(col_id <= row_id, 0.0, -jnp.inf)
    neg_ref[1] = jnp.where(col_id >= row_id, 0.0, -jnp.inf)

    def to_col(row):
        return jnp.sum(jnp.where(row_id == col_id, row, 0.0), axis=1, keepdims=True)

    def scores(h, d, c):
        hs = slice(h * HEAD_DIM, (h + 1) * HEAD_DIM)
        tok = _chunk_rows(c)
        rows = rows_ref[2 * h + d, c]
        m_col = to_col(rows[ROW_M:ROW_M + 1])
        g_col = jnp.exp2(rows[ROW_MIN:ROW_MIN + 1] - m_col)
        dmat = jnp.exp2(rows[ROW_A:ROW_A + 1] - m_col + neg_ref[d])
        q = q_ref[tok, hs]
        kt = kt_ref[hs, tok]
        lhs = jnp.concatenate([(_dot(q, kt) * dmat).astype(BF16),
                               (q.astype(F32) * g_col).astype(BF16)], axis=1)
        kw = (kt.astype(F32) * rows[ROW_W:ROW_W + 1]).astype(BF16)
        return lhs, kw, rows

    def readout(h, d, c, lhs, kw, rows):
        hs = slice(h * HEAD_DIM, (h + 1) * HEAD_DIM)
        v_aug = _v_aug(v_ref[_chunk_rows(c), hs])
        s_old = s_ref[2 * h + d]
        tot = _dot(lhs, jnp.concatenate([v_aug, s_old.astype(BF16)], axis=0))
        dec = rows[ROW_DECAY:ROW_DECAY + 1]
        s_ref[2 * h + d] = jnp.concatenate([dec, dec], axis=1) * s_old + _dot(kw, v_aug)
        return tot

    def normalise(tot, rows):
        e_col = to_col(rows[ROW_E:ROW_E + 1])
        return tot[:, :HEAD_DIM] / jnp.maximum(jnp.abs(tot[:, HEAD_DIM:]), e_col)

    def iteration(i, accumulate):
        scans = [(h, d, (nc - 1 - i) if d else i) for h in range(N_HEADS) for d in range(2)]
        staged = [scores(*s) for s in scans]
        tots = [readout(*s, *st) for s, st in zip(scans, staged)]
        for (h, d, c), tot, st in zip(scans, tots, staged):
            hh = normalise(tot, st[2])
            dst = (_chunk_rows(c), slice(h * HEAD_DIM, (h + 1) * HEAD_DIM))
            out_ref[dst] = out_ref[dst] + hh if accumulate else hh

    def first_half(i, carry):
        iteration(i, False)
        return carry

    def second_half(i, carry):
        iteration(i, True)
        return carry

    lax.fori_loop(0, nc // 2, first_half, 0, unroll=2)
    lax.fori_loop(nc // 2, nc, second_half, 0, unroll=2)


def _mlstm(qvo, kt, gt, cv, ckt, cgt):
    nb, seq, _ = qvo.shape
    ctx_len = cv.shape[1]
    assert seq % (2 * CHUNK) == 0 and ctx_len % CHUNK == 0
    nc, ncc = seq // CHUNK, ctx_len // CHUNK
    qvo_blk = lambda j: pl.BlockSpec((None, seq, MLSTM_W), lambda b: (b, 0, j))
    whole = lambda *s: pl.BlockSpec((None,) + s, lambda b: (b,) + (0,) * len(s))
    return pl.pallas_call(
        _mlstm_kernel,
        grid=(nb,),
        in_specs=[qvo_blk(0), whole(MLSTM_W, seq), qvo_blk(1), whole(N_GATES, nc, CHUNK),
                  whole(MLSTM_W, ctx_len), whole(ctx_len, MLSTM_W), whole(N_GATES, ncc, CHUNK)],
        out_specs=whole(seq, MLSTM_W),
        out_shape=jax.ShapeDtypeStruct((nb, seq, MLSTM_W), F32),
        scratch_shapes=[pltpu.VMEM((2 * N_HEADS, HEAD_DIM, 2 * HEAD_DIM), F32),
                        pltpu.VMEM((2 * N_HEADS, nc, N_ROWS, CHUNK), F32),
                        pltpu.VMEM((2, CHUNK, CHUNK), F32)],
        compiler_params=pltpu.CompilerParams(dimension_semantics=("arbitrary",),
                                             vmem_limit_bytes=VMEM_LIMIT),
        name="mlstm",
    )(qvo, kt, qvo, gt.reshape(nb, N_GATES, nc, CHUNK), ckt, cv, cgt.reshape(nb, N_GATES, ncc, CHUNK))


def _mlstm_out(h, o, mh_norm):
    hn = jnp.concatenate([_rms(h[:, j * HEAD_DIM:(j + 1) * HEAD_DIM]) for j in range(N_HEADS)], axis=1)
    return (jax.nn.sigmoid(o.astype(F32)) * (hn * mh_norm)).astype(BF16)


def _out_ffn_kernel(x1_ref, conv_ref, h_ref, o_ref, mod_ref, g_ref, mhn_ref, wo_ref, wup_ref, wdn_ref, out_ref):
    rows = _sub_rows(x1_ref.shape[0])
    hms = [_mlstm_out(h_ref[r, :], o_ref[r, :], mhn_ref[...]) for r in rows]
    ys = [_dot(conv_ref[r, :], wo_ref[0:CONV_W, :]) + _dot(hm, wo_ref[CONV_W:, :]) for r, hm in zip(rows, hms)]
    x2s = [x1_ref[r, :] + mod_ref[5] * (_rms(y) * g_ref[3]) for r, y in zip(rows, ys)]
    hs = [_ffn_in(x2, g_ref[4], mod_ref[6], mod_ref[7]) for x2 in x2s]
    accs = [_swiglu(h, wup_ref, wdn_ref) for h in hs]
    for r, x2, acc in zip(rows, x2s, accs):
        out_ref[r, :] = _ffn_out(x2, acc, g_ref[5], mod_ref[8])


def _out_ffn(x1, conv, hsum, qvo, mod4, g3, mhn, wo, wup, wdn, seq, tm):
    n, d = x1.shape
    tpb = seq // tm
    tile = lambda w: pl.BlockSpec((tm, w), lambda i: (i, 0))
    return pl.pallas_call(
        _out_ffn_kernel,
        grid=(n // tm,),
        in_specs=[tile(d), tile(CONV_W), tile(MLSTM_W),
                  pl.BlockSpec((tm, MLSTM_W), lambda i: (i, 2)),
                  pl.BlockSpec((None, N_MOD, 1, d), lambda i: (i // tpb, 0, 0, 0)),
                  _resident(g3.shape), _resident(mhn.shape), _resident(wo.shape), _resident(wup.shape),
                  _resident(wdn.shape)],
        out_specs=tile(d),
        out_shape=jax.ShapeDtypeStruct((n, d), F32),
        compiler_params=pltpu.CompilerParams(dimension_semantics=("arbitrary",),
                                             vmem_limit_bytes=VMEM_LIMIT),
        name="outproj_ffn2",
    )(x1, conv, hsum, qvo, mod4, g3, mhn, wo, wup, wdn)


def kernel(x, c, ctx, c_ctx, w_mod, b_mod, norm_g, ffn1_up, ffn1_down, ffn2_up, ffn2_down,
           w_in, b_gates, conv_w, mh_norm, w_out):
    nb, seq, d = x.shape
    ctx_len = ctx.shape[1]
    depth = w_mod.shape[0]
    assert depth == 1, "only the single (last) layer configuration is implemented"
    assert nb + 1 <= MOD_ROWS and seq % GRID_W == 0
    tm_in, tm_out = min(1024, seq), min(1024, seq)
    assert tm_in % SUB_ROWS == 0 and tm_out % SUB_ROWS == 0 and SUB_ROWS % GRID_W == 0
    conv_cols = 3 * CONV_W

    cvec = jnp.concatenate([c, c_ctx[None], jnp.zeros((MOD_ROWS - nb - 1, d), F32)], axis=0)
    mod4 = _modulation(cvec, w_mod[0], b_mod[0]).reshape(MOD_ROWS, N_MOD, 1, d)

    g3 = norm_g[0].reshape(6, 1, d)
    w1u, w1d = ffn1_up[0].astype(BF16), ffn1_down[0].astype(BF16)
    w2u, w2d = ffn2_up[0].astype(BF16), ffn2_down[0].astype(BF16)
    win = w_in[0]
    seg = lambda j: win[:, conv_cols + j * MLSTM_W:conv_cols + (j + 1) * MLSTM_W]
    wc = win[:, :conv_cols].astype(BF16)
    wqvo = jnp.concatenate([seg(0), seg(2), seg(3)], axis=1).astype(BF16)
    wv = seg(2).astype(BF16)
    wkt = seg(1).T.astype(BF16)
    wg = win[:, conv_cols + 4 * MLSTM_W:].T.astype(BF16)
    bg = b_gates[0].reshape(N_GATES, 1)
    cw3 = conv_w[0].reshape(3, 1, CONV_W)
    wo = w_out[0].astype(BF16)
    mh = mh_norm[0].reshape(1, MLSTM_W)

    x1, conv, qvo, kt, gt = _ffn_proj(x.reshape(nb * seq, d), mod4, g3, w1u, w1d, wc, wqvo, wkt, wg, bg,
                                      cw3, seq, tm_in)
    cv, ckt, cgt = _ffn_proj_ctx(ctx.reshape(nb * ctx_len, d), mod4, g3, w1u, w1d, wv, wkt, wg, bg,
                                 ctx_len, nb)
    hsum = _mlstm(qvo.reshape(nb, seq, 3 * MLSTM_W), kt, gt, cv.reshape(nb, ctx_len, MLSTM_W), ckt, cgt)
    out = _out_ffn(x1, conv, hsum.reshape(nb * seq, MLSTM_W), qvo, mod4, g3, mh, wo, w2u, w2d, seq, tm_out)
    return out.reshape(nb, seq, d)
```

```python
import jax
import jax.numpy as jnp
from jax import lax
from jax.experimental import pallas as pl
from jax.experimental.pallas import tpu as pltpu

F32 = jnp.float32
BF16 = jnp.bfloat16

GRID_W = 64
CONV_W = 512
N_HEADS = 4
HEAD_DIM = 128
MLSTM_W = N_HEADS * HEAD_DIM
CHUNK = 128
N_MOD = 9
EPS = 1e-6
LOG2E = 1.4426950408889634
N_GATES = 4 * N_HEADS
MOD_ROWS = 24
VMEM_LIMIT = 60 * 1024 * 1024

FFN_CHUNK = 512
STAGE_BYTES = 512 * 1024
BF16_SUBLANES = 16
SUB_ROWS = 256
CTX_PER_STEP = 2

ROW_A, ROW_M, ROW_E, ROW_W, ROW_DECAY, ROW_MIN, N_ROWS = 0, 1, 2, 3, 4, 5, 8


def _dot(a, b):
    return jnp.dot(a, b, preferred_element_type=F32)


def _dot_nt(a, b):
    return lax.dot_general(a, b, (((1,), (1,)), ((), ())), preferred_element_type=F32)


def _rms(x):
    return x * lax.rsqrt(jnp.mean(x * x, axis=-1, keepdims=True) + EPS)


def _resident(shape):
    nd = len(shape)
    return pl.BlockSpec(shape, lambda *_: (0,) * nd, pipeline_mode=pl.Buffered(1))


def _mod_kernel(c_ref, w_ref, b_ref, o_ref):
    cv = c_ref[...]
    s = (cv * jax.nn.sigmoid(cv)).astype(BF16)
    o_ref[...] = _dot(s, w_ref[...].astype(BF16)) + b_ref[...]


def _modulation(cvec, w_mod, b_mod):
    d = cvec.shape[1]
    n = w_mod.shape[1]
    bn = n // N_MOD
    return pl.pallas_call(
        _mod_kernel,
        grid=(N_MOD,),
        in_specs=[pl.BlockSpec((MOD_ROWS, d), lambda j: (0, 0)),
                  pl.BlockSpec((d, bn), lambda j: (0, j)),
                  pl.BlockSpec((1, bn), lambda j: (0, j))],
        out_specs=pl.BlockSpec((MOD_ROWS, bn), lambda j: (0, j)),
        out_shape=jax.ShapeDtypeStruct((MOD_ROWS, n), F32),
        compiler_params=pltpu.CompilerParams(dimension_semantics=("arbitrary",),
                                             vmem_limit_bytes=VMEM_LIMIT),
        name="modulation",
    )(cvec, w_mod, b_mod.reshape(1, n))


def _ffn_in(x, g_pre, shift, scale):
    return (_rms(x) * g_pre * (1.0 + scale) + shift).astype(BF16)


def _swiglu(h, wup_ref, wdn_ref):
    ffn = wdn_ref.shape[0]
    chunks = [(c0, min(FFN_CHUNK, ffn - c0)) for c0 in range(0, ffn, FFN_CHUNK)]

    def up(c0, cw):
        return _dot(h, wup_ref[:, c0:c0 + cw]), _dot(h, wup_ref[:, ffn + c0:ffn + c0 + cw])

    acc = None
    ab = up(*chunks[0])
    for i, (c0, cw) in enumerate(chunks):
        a, b = ab
        if i + 1 < len(chunks):
            ab = up(*chunks[i + 1])
        act = (a * jax.nn.sigmoid(a) * b).astype(BF16)
        part = _dot(act, wdn_ref[c0:c0 + cw, :])
        acc = part if acc is None else acc + part
    return acc


def _ffn_out(x, acc, g_post, gate):
    return x + 0.5 * gate * (_rms(acc) * g_post)


def _sub_rows(n_rows):
    return [pl.ds(r, SUB_ROWS) for r in range(0, n_rows, SUB_ROWS)]


def _log_sigmoid(x):
    return jnp.minimum(x, 0.0) - jnp.log1p(jnp.exp(-jnp.abs(x)))


def _gates_t(h, wg_ref, bg_ref):
    g = _dot_nt(wg_ref[...], h) + bg_ref[...]
    half = N_GATES // 2
    return jnp.concatenate([g[:half], _log_sigmoid(g[half:])], axis=0)


def _mixer_in(x1, g_ref, mod_ref):
    return (_rms(x1) * g_ref[2] * (1.0 + mod_ref[4]) + mod_ref[3]).astype(BF16)


def _ffn_proj_kernel(x_ref, mod_ref, g_ref, wup_ref, wdn_ref, wc_ref, wqvo_ref, wkt_ref, wg_ref, bg_ref,
                     cw_ref, x1_ref, conv_ref, qvo_ref, kt_ref, gt_ref):
    rows = _sub_rows(x_ref.shape[0])
    hs = [_ffn_in(x_ref[r, :], g_ref[0], mod_ref[0], mod_ref[1]) for r in rows]
    accs = [_swiglu(h, wup_ref, wdn_ref) for h in hs]
    h2s = []
    for r, acc in zip(rows, accs):
        x1 = _ffn_out(x_ref[r, :], acc, g_ref[1], mod_ref[2])
        x1_ref[r, :] = x1
        h2s.append(_mixer_in(x1, g_ref, mod_ref))

    for r, h in zip(rows, h2s):
        sub = h.shape[0]
        bg = _dot(h, wc_ref[:, 0:CONV_W])
        cu = _dot(h, wc_ref[:, CONV_W:2 * CONV_W]) * _dot(h, wc_ref[:, 2 * CONV_W:3 * CONV_W])
        col = lax.broadcasted_iota(jnp.int32, (sub, 1), 0) % GRID_W
        prev = jnp.where(col == 0, 0.0, pltpu.roll(cu, 1, 0))
        nxt = jnp.where(col == GRID_W - 1, 0.0, pltpu.roll(cu, sub - 1, 0))
        conv = bg * (cw_ref[0] * prev + cw_ref[1] * cu + cw_ref[2] * nxt)
        conv_ref[r, :] = conv.astype(BF16)

        for j in range(3):
            cols = slice(j * MLSTM_W, (j + 1) * MLSTM_W)
            qvo_ref[r, cols] = _dot(h, wqvo_ref[:, cols]).astype(BF16)
        kt_ref[:, r] = (_dot_nt(wkt_ref[...], h) * (HEAD_DIM ** -0.5)).astype(BF16)
        gt_ref[:, r] = _gates_t(h, wg_ref, bg_ref)


def _ffn_proj(x2d, mod4, g3, wup, wdn, wc, wqvo, wkt, wg, bg, cw3, seq, tm):
    n, d = x2d.shape
    tpb = seq // tm
    nb = n // seq
    tile = lambda w: pl.BlockSpec((tm, w), lambda i: (i, 0))
    tile_t = lambda r: pl.BlockSpec((None, r, tm), lambda i: (i // tpb, 0, i % tpb))
    return pl.pallas_call(
        _ffn_proj_kernel,
        grid=(n // tm,),
        in_specs=[tile(d),
                  pl.BlockSpec((None, N_MOD, 1, d), lambda i: (i // tpb, 0, 0, 0)),
                  _resident(g3.shape), _resident(wup.shape), _resident(wdn.shape),
                  _resident(wc.shape), _resident(wqvo.shape), _resident(wkt.shape), _resident(wg.shape),
                  _resident(bg.shape), _resident(cw3.shape)],
        out_specs=[tile(d), tile(CONV_W), tile(3 * MLSTM_W), tile_t(MLSTM_W), tile_t(N_GATES)],
        out_shape=[jax.ShapeDtypeStruct((n, d), F32),
                   jax.ShapeDtypeStruct((n, CONV_W), BF16),
                   jax.ShapeDtypeStruct((n, 3 * MLSTM_W), BF16),
                   jax.ShapeDtypeStruct((nb, MLSTM_W, seq), BF16),
                   jax.ShapeDtypeStruct((nb, N_GATES, seq), F32)],
        compiler_params=pltpu.CompilerParams(dimension_semantics=("arbitrary",),
                                             vmem_limit_bytes=VMEM_LIMIT),
        name="ffn1_inproj",
    )(x2d, mod4, g3, wup, wdn, wc, wqvo, wkt, wg, bg, cw3)


def _ffn_proj_ctx_kernel(x_ref, mod_ref, g_ref, wup_ref, wdn_ref, wv_ref, wkt_ref, wg_ref, bg_ref,
                         v_ref, kt_ref, gt_ref):
    n_sub, ctx_len = kt_ref.shape[0], kt_ref.shape[2]
    rows = [pl.ds(r * ctx_len, ctx_len) for r in range(n_sub)]
    hs = [_ffn_in(x_ref[r, :], g_ref[0], mod_ref[0], mod_ref[1]) for r in rows]
    accs = [_swiglu(h, wup_ref, wdn_ref) for h in hs]
    h2s = [_mixer_in(_ffn_out(x_ref[r, :], acc, g_ref[1], mod_ref[2]), g_ref, mod_ref)
           for r, acc in zip(rows, accs)]
    for j, (r, h) in enumerate(zip(rows, h2s)):
        v_ref[r, :] = _dot(h, wv_ref[...]).astype(BF16)
        kt_ref[j] = (_dot_nt(wkt_ref[...], h) * (HEAD_DIM ** -0.5)).astype(BF16)
        gt_ref[j] = _gates_t(h, wg_ref, bg_ref)


def _ffn_proj_ctx(c2d, mod4, g3, wup, wdn, wv, wkt, wg, bg, ctx_len, ctx_row):
    n, d = c2d.shape
    assert (n // ctx_len) % CTX_PER_STEP == 0
    tm = CTX_PER_STEP * ctx_len
    tile_t = lambda r: pl.BlockSpec((CTX_PER_STEP, r, ctx_len), lambda i: (i, 0, 0))
    return pl.pallas_call(
        _ffn_proj_ctx_kernel,
        grid=(n // tm,),
        in_specs=[pl.BlockSpec((tm, d), lambda i: (i, 0)),
                  pl.BlockSpec((None, N_MOD, 1, d), lambda i: (ctx_row, 0, 0, 0)),
                  _resident(g3.shape), _resident(wup.shape), _resident(wdn.shape),
                  _resident(wv.shape), _resident(wkt.shape), _resident(wg.shape), _resident(bg.shape)],
        out_specs=[pl.BlockSpec((tm, MLSTM_W), lambda i: (i, 0)), tile_t(MLSTM_W), tile_t(N_GATES)],
        out_shape=[jax.ShapeDtypeStruct((n, MLSTM_W), BF16),
                   jax.ShapeDtypeStruct((n // ctx_len, MLSTM_W, ctx_len), BF16),
                   jax.ShapeDtypeStruct((n // ctx_len, N_GATES, ctx_len), F32)],
        compiler_params=pltpu.CompilerParams(dimension_semantics=("arbitrary",),
                                             vmem_limit_bytes=VMEM_LIMIT),
        name="ffn1_inproj_ctx",
    )(c2d, mod4, g3, wup, wdn, wv, wkt, wg, bg)


def _lane_scan(x, op, reverse):
    pos = lax.broadcasted_iota(jnp.int32, x.shape, 1)
    d = 1
    while d < CHUNK:
        if reverse:
            x = jnp.where(pos < CHUNK - d, op(x, pltpu.roll(x, CHUNK - d, 1)), x)
        else:
            x = jnp.where(pos >= d, op(x, pltpu.roll(x, d, 1)), x)
        d *= 2
    return x


def _lane_pick(x, lane):
    idx = lax.broadcasted_iota(jnp.int32, x.shape, 1)
    return jnp.sum(jnp.where(idx == lane, x, 0.0), axis=1, keepdims=True)


def _chunk_rows(c):
    if isinstance(c, int):
        return pl.ds(c * CHUNK, CHUNK)
    return pl.ds(pl.multiple_of(c * CHUNK, CHUNK), CHUNK)


def _gate_scan(ig, lf, m, reverse):
    n = ig.shape[0]
    b = _lane_scan(lf, jnp.add, reverse)
    a = ig - b
    b_last = _lane_pick(b, 0 if reverse else CHUNK - 1)
    wmax = b_last + jnp.max(a, axis=1, keepdims=True)
    sub = lax.broadcasted_iota(jnp.int32, (n, 1), 0)
    m_in = jnp.zeros((n, 1), F32)
    m_out = jnp.zeros((n, 1), F32)
    for c in (reversed(range(n)) if reverse else range(n)):
        m_new = jnp.maximum(b_last[c:c + 1] + m, wmax[c:c + 1])
        m_in = jnp.where(sub == c, m, m_in)
        m_out = jnp.where(sub == c, m_new, m_out)
        m = m_new
    big_m = jnp.maximum(m_in, _lane_scan(a, jnp.maximum, reverse))
    e = jnp.exp(-(b + big_m))
    w = jnp.exp(a + b_last - m_out)
    decay = jnp.exp(b_last + m_in - m_out)
    return (a, big_m, e, w, decay, m_in), m


def _v_aug(v):
    return jnp.concatenate([v, jnp.ones((CHUNK, HEAD_DIM), BF16)], axis=1)


def _mlstm_kernel(q_ref, kt_ref, v_ref, g_ref, ckt_ref, cv_ref, cg_ref, out_ref, s_ref, rows_ref, neg_ref):
    nc = kt_ref.shape[1] // CHUNK
    ncc = ckt_ref.shape[1] // CHUNK

    for h in range(N_HEADS):
        hs = slice(h * HEAD_DIM, (h + 1) * HEAD_DIM)
        for d in range(2):
            hd = 2 * h + d
            rev = d == 1
            i_row, f_row = d * N_HEADS + h, (2 + d) * N_HEADS + h
            (_, _, _, w, decay, _), m = _gate_scan(cg_ref[i_row], cg_ref[f_row], jnp.zeros((1, 1), F32), rev)
            s = jnp.zeros((HEAD_DIM, 2 * HEAD_DIM), F32)
            for c in (reversed(range(ncc)) if rev else range(ncc)):
                tok = _chunk_rows(c)
                kw = (ckt_ref[hs, tok].astype(F32) * w[c:c + 1]).astype(BF16)
                s = decay[c:c + 1] * s + _dot(kw, _v_aug(cv_ref[tok, hs]))
            s_ref[hd] = s
            (a, big_m, e, w, decay, m_in), _ = _gate_scan(g_ref[i_row], g_ref[f_row], m, rev)
            decay = jnp.broadcast_to(decay, (nc, CHUNK))
            m_in = jnp.broadcast_to(m_in, (nc, CHUNK))
            vals = ((ROW_A, a * LOG2E), (ROW_M, big_m * LOG2E), (ROW_E, e), (ROW_W, w),
                    (ROW_DECAY, decay), (ROW_MIN, m_in * LOG2E))
            for c in range(nc):
                for r, val in vals:
                    rows_ref[hd, c, r:r + 1, :] = val[c:c + 1]

    row_id = lax.broadcasted_iota(jnp.int32, (CHUNK, CHUNK), 0)
    col_id = lax.broadcasted_iota(jnp.int32, (CHUNK, CHUNK), 1)
    neg_ref[0] = jnp.where(col_id <= row_id, 0.0, -jnp.inf)
    neg_ref[1] = jnp.where(col_id >= row_id, 0.0, -jnp.inf)

    def to_col(row):
        return jnp.sum(jnp.where(row_id == col_id, row, 0.0), axis=1, keepdims=True)

    def scores(h, d, c):
        hs = slice(h * HEAD_DIM, (h + 1) * HEAD_DIM)
        tok = _chunk_rows(c)
        rows = rows_ref[2 * h + d, c]
        m_col = to_col(rows[ROW_M:ROW_M + 1])
        g_col = jnp.exp2(rows[ROW_MIN:ROW_MIN + 1] - m_col)
        dmat = jnp.exp2(rows[ROW_A:ROW_A + 1] - m_col + neg_ref[d])
        q = q_ref[tok, hs]
        kt = kt_ref[hs, tok]
        lhs = jnp.concatenate([(_dot(q, kt) * dmat).astype(BF16),
                               (q.astype(F32) * g_col).astype(BF16)], axis=1)
        kw = (kt.astype(F32) * rows[ROW_W:ROW_W + 1]).astype(BF16)
        return lhs, kw, rows

    def readout(h, d, c, lhs, kw, rows):
        hs = slice(h * HEAD_DIM, (h + 1) * HEAD_DIM)
        v_aug = _v_aug(v_ref[_chunk_rows(c), hs])
        s_old = s_ref[2 * h + d]
        tot = _dot(lhs, jnp.concatenate([v_aug, s_old.astype(BF16)], axis=0))
        dec = rows[ROW_DECAY:ROW_DECAY + 1]
        s_ref[2 * h + d] = jnp.concatenate([dec, dec], axis=1) * s_old + _dot(kw, v_aug)
        return tot

    def normalise(tot, rows):
        e_col = to_col(rows[ROW_E:ROW_E + 1])
        return tot[:, :HEAD_DIM] / jnp.maximum(jnp.abs(tot[:, HEAD_DIM:]), e_col)

    def iteration(i, accumulate):
        scans = [(h, d, (nc - 1 - i) if d else i) for h in range(N_HEADS) for d in range(2)]
        staged = [scores(*s) for s in scans]
        tots = [readout(*s, *st) for s, st in zip(scans, staged)]
        for (h, d, c), tot, st in zip(scans, tots, staged):
            hh = normalise(tot, st[2])
            dst = (_chunk_rows(c), slice(h * HEAD_DIM, (h + 1) * HEAD_DIM))
            out_ref[dst] = out_ref[dst] + hh if accumulate else hh

    def first_half(i, carry):
        iteration(i, False)
        return carry

    def second_half(i, carry):
        iteration(i, True)
        return carry

    lax.fori_loop(0, nc // 2, first_half, 0, unroll=2)
    lax.fori_loop(nc // 2, nc, second_half, 0, unroll=2)


def _mlstm(qvo, kt, gt, cv, ckt, cgt):
    nb, seq, _ = qvo.shape
    ctx_len = cv.shape[1]
    assert seq % (2 * CHUNK) == 0 and ctx_len % CHUNK == 0
    nc, ncc = seq // CHUNK, ctx_len // CHUNK
    qvo_blk = lambda j: pl.BlockSpec((None, seq, MLSTM_W), lambda b: (b, 0, j))
    whole = lambda *s: pl.BlockSpec((None,) + s, lambda b: (b,) + (0,) * len(s))
    return pl.pallas_call(
        _mlstm_kernel,
        grid=(nb,),
        in_specs=[qvo_blk(0), whole(MLSTM_W, seq), qvo_blk(1), whole(N_GATES, nc, CHUNK),
                  whole(MLSTM_W, ctx_len), whole(ctx_len, MLSTM_W), whole(N_GATES, ncc, CHUNK)],
        out_specs=whole(seq, MLSTM_W),
        out_shape=jax.ShapeDtypeStruct((nb, seq, MLSTM_W), F32),
        scratch_shapes=[pltpu.VMEM((2 * N_HEADS, HEAD_DIM, 2 * HEAD_DIM), F32),
                        pltpu.VMEM((2 * N_HEADS, nc, N_ROWS, CHUNK), F32),
                        pltpu.VMEM((2, CHUNK, CHUNK), F32)],
        compiler_params=pltpu.CompilerParams(dimension_semantics=("arbitrary",),
                                             vmem_limit_bytes=VMEM_LIMIT),
        name="mlstm",
    )(qvo, kt, qvo, gt.reshape(nb, N_GATES, nc, CHUNK), ckt, cv, cgt.reshape(nb, N_GATES, ncc, CHUNK))


def _mlstm_out(h, o, mh_norm):
    hn = jnp.concatenate([_rms(h[:, j * HEAD_DIM:(j + 1) * HEAD_DIM]) for j in range(N_HEADS)], axis=1)
    return (jax.nn.sigmoid(o.astype(F32)) * (hn * mh_norm)).astype(BF16)


def _load_bf16(src_hbm, dst, stage, sem, rows):
    n_chunks = src_hbm.shape[0] // rows

    def copy(c, slot):
        return pltpu.make_async_copy(src_hbm.at[pl.ds(c * rows, rows)], stage.at[slot, pl.ds(0, rows)],
                                     sem.at[slot])

    copy(0, 0).start()

    def body(c, carry):
        slot = c % 2

        @pl.when(c + 1 < n_chunks)
        def _():
            copy(c + 1, 1 - slot).start()

        copy(c, slot).wait()
        dst[pl.ds(pl.multiple_of(c * rows, rows), rows), :] = stage[slot, pl.ds(0, rows), :].astype(BF16)
        return carry

    lax.fori_loop(0, n_chunks, body, 0)


def _out_ffn_kernel(x1_ref, conv_ref, h_ref, o_ref, mod_ref, g_ref, mhn_ref, wo_hbm, wup_hbm, wdn_hbm, out_ref,
                    wo_ref, wup_ref, wdn_ref, stage_wide, stage_tall, sem_wide, sem_tall):
    @pl.when(pl.program_id(0) == 0)
    def _():
        _load_bf16(wo_hbm, wo_ref, stage_tall, sem_tall, stage_tall.shape[1])
        _load_bf16(wup_hbm, wup_ref, stage_wide, sem_wide, stage_wide.shape[1])
        _load_bf16(wdn_hbm, wdn_ref, stage_tall, sem_tall, stage_tall.shape[1])

    rows = _sub_rows(x1_ref.shape[0])
    hms = [_mlstm_out(h_ref[r, :], o_ref[r, :], mhn_ref[...]) for r in rows]
    ys = [_dot(conv_ref[r, :], wo_ref[0:CONV_W, :]) + _dot(hm, wo_ref[CONV_W:, :]) for r, hm in zip(rows, hms)]
    x2s = [x1_ref[r, :] + mod_ref[5] * (_rms(y) * g_ref[3]) for r, y in zip(rows, ys)]
    hs = [_ffn_in(x2, g_ref[4], mod_ref[6], mod_ref[7]) for x2 in x2s]
    accs = [_swiglu(h, wup_ref, wdn_ref) for h in hs]
    for r, x2, acc in zip(rows, x2s, accs):
        out_ref[r, :] = _ffn_out(x2, acc, g_ref[5], mod_ref[8])


def _stage_rows(shape):
    rows = BF16_SUBLANES
    while 2 * rows * shape[1] * 4 <= STAGE_BYTES and shape[0] % (2 * rows) == 0:
        rows *= 2
    assert shape[0] % rows == 0
    return rows


def _out_ffn(x1, conv, hsum, qvo, mod4, g3, mhn, wo, wup, wdn, seq, tm):
    n, d = x1.shape
    tpb = seq // tm
    tile = lambda w: pl.BlockSpec((tm, w), lambda i: (i, 0))
    return pl.pallas_call(
        _out_ffn_kernel,
        grid=(n // tm,),
        in_specs=[tile(d), tile(CONV_W), tile(MLSTM_W),
                  pl.BlockSpec((tm, MLSTM_W), lambda i: (i, 2)),
                  pl.BlockSpec((None, N_MOD, 1, d), lambda i: (i // tpb, 0, 0, 0)),
                  _resident(g3.shape), _resident(mhn.shape)] + [pl.BlockSpec(memory_space=pl.ANY)] * 3,
        out_specs=tile(d),
        out_shape=jax.ShapeDtypeStruct((n, d), F32),
        scratch_shapes=[pltpu.VMEM(wo.shape, BF16), pltpu.VMEM(wup.shape, BF16), pltpu.VMEM(wdn.shape, BF16),
                        pltpu.VMEM((2, _stage_rows(wup.shape), wup.shape[1]), F32),
                        pltpu.VMEM((2, min(_stage_rows(wo.shape), _stage_rows(wdn.shape)), d), F32),
                        pltpu.SemaphoreType.DMA((2,)), pltpu.SemaphoreType.DMA((2,))],
        compiler_params=pltpu.CompilerParams(dimension_semantics=("arbitrary",),
                                             vmem_limit_bytes=VMEM_LIMIT),
        name="outproj_ffn2",
    )(x1, conv, hsum, qvo, mod4, g3, mhn, wo, wup, wdn)


def kernel(x, c, ctx, c_ctx, w_mod, b_mod, norm_g, ffn1_up, ffn1_down, ffn2_up, ffn2_down,
           w_in, b_gates, conv_w, mh_norm, w_out):
    nb, seq, d = x.shape
    ctx_len = ctx.shape[1]
    depth = w_mod.shape[0]
    assert depth == 1, "only the single (last) layer configuration is implemented"
    assert nb + 1 <= MOD_ROWS and seq % GRID_W == 0
    tm_in, tm_out = min(512, seq), min(1024, seq)
    assert tm_in % SUB_ROWS == 0 and tm_out % SUB_ROWS == 0 and SUB_ROWS % GRID_W == 0
    conv_cols = 3 * CONV_W

    cvec = jnp.concatenate([c, c_ctx[None], jnp.zeros((MOD_ROWS - nb - 1, d), F32)], axis=0)
    mod4 = _modulation(cvec, w_mod[0], b_mod[0]).reshape(MOD_ROWS, N_MOD, 1, d)

    g3 = norm_g[0].reshape(6, 1, d)
    w1u, w1d = ffn1_up[0].astype(BF16), ffn1_down[0].astype(BF16)
    win = w_in[0]
    seg = lambda j: win[:, conv_cols + j * MLSTM_W:conv_cols + (j + 1) * MLSTM_W]
    wc = win[:, :conv_cols].astype(BF16)
    wqvo = jnp.concatenate([seg(0), seg(2), seg(3)], axis=1).astype(BF16)
    wv = seg(2).astype(BF16)
    wkt = seg(1).T.astype(BF16)
    wg = win[:, conv_cols + 4 * MLSTM_W:].T.astype(BF16)
    bg = b_gates[0].reshape(N_GATES, 1)
    cw3 = conv_w[0].reshape(3, 1, CONV_W)
    mh = mh_norm[0].reshape(1, MLSTM_W)

    x1, conv, qvo, kt, gt = _ffn_proj(x.reshape(nb * seq, d), mod4, g3, w1u, w1d, wc, wqvo, wkt, wg, bg,
                                      cw3, seq, tm_in)
    cv, ckt, cgt = _ffn_proj_ctx(ctx.reshape(nb * ctx_len, d), mod4, g3, w1u, w1d, wv, wkt, wg, bg,
                                 ctx_len, nb)
    hsum = _mlstm(qvo.reshape(nb, seq, 3 * MLSTM_W), kt, gt, cv.reshape(nb, ctx_len, MLSTM_W), ckt, cgt)
    out = _out_ffn(x1, conv, hsum.reshape(nb * seq, MLSTM_W), qvo, mod4, g3, mh, w_out[0], ffn2_up[0], ffn2_down[0],
                   seq, tm_out)
    return out.reshape(nb, seq, d)
```

```python
import jax
import jax.numpy as jnp
from jax import lax
from jax.experimental import pallas as pl
from jax.experimental.pallas import tpu as pltpu

F32 = jnp.float32
BF16 = jnp.bfloat16

GRID_W = 64
CONV_W = 512
N_HEADS = 4
HEAD_DIM = 128
MLSTM_W = N_HEADS * HEAD_DIM
CHUNK = 128
N_MOD = 9
EPS = 1e-6
LOG2E = 1.4426950408889634
N_GATES = 4 * N_HEADS
MOD_ROWS = 24
VMEM_LIMIT = 56 * 1024 * 1024

FFN_CHUNK = 512
MXU_COLS = 256
BF16_SUBLANES = 16
SUB_ROWS = 256
CTX_PER_STEP = 2

ROW_A, ROW_M, ROW_E, ROW_W, ROW_DECAY, ROW_MIN, N_ROWS = 0, 1, 2, 3, 4, 5, 8


def _dot(a, b):
    return jnp.dot(a, b, preferred_element_type=F32)


def _dot_nt(a, b):
    return lax.dot_general(a, b, (((1,), (1,)), ((), ())), preferred_element_type=F32)


def _rms(x):
    return x * lax.rsqrt(jnp.mean(x * x, axis=-1, keepdims=True) + EPS)


def _params():
    return pltpu.CompilerParams(dimension_semantics=("arbitrary",), vmem_limit_bytes=VMEM_LIMIT)


def _resident(shape):
    nd = len(shape)
    return pl.BlockSpec(shape, lambda *_: (0,) * nd, pipeline_mode=pl.Buffered(1))


def _mod_kernel(c_ref, w_ref, b_ref, o_ref):
    cv = c_ref[...]
    s = (cv * jax.nn.sigmoid(cv)).astype(BF16)
    o_ref[...] = _dot(s, w_ref[...].astype(BF16)) + b_ref[...]


def _modulation(cvec, w_mod, b_mod):
    d = cvec.shape[1]
    n = w_mod.shape[1]
    bn = n // N_MOD
    return pl.pallas_call(
        _mod_kernel,
        grid=(N_MOD,),
        in_specs=[pl.BlockSpec((MOD_ROWS, d), lambda j: (0, 0)),
                  pl.BlockSpec((d, bn), lambda j: (0, j)),
                  pl.BlockSpec((1, bn), lambda j: (0, j))],
        out_specs=pl.BlockSpec((MOD_ROWS, bn), lambda j: (0, j)),
        out_shape=jax.ShapeDtypeStruct((MOD_ROWS, n), F32),
        compiler_params=_params(),
        name="modulation",
    )(cvec, w_mod, b_mod.reshape(1, n))


def _ffn_in(x, g_pre, shift, scale):
    return (_rms(x) * g_pre * (1.0 + scale) + shift).astype(BF16)


def _swiglu(h, wup_ref, wdn_ref):
    ffn = wdn_ref.shape[0]
    chunks = [(c0, min(FFN_CHUNK, ffn - c0)) for c0 in range(0, ffn, FFN_CHUNK)]

    def up(c0, cw):
        return _dot(h, wup_ref[:, c0:c0 + cw]), _dot(h, wup_ref[:, ffn + c0:ffn + c0 + cw])

    acc = None
    ab = up(*chunks[0])
    for i, (c0, cw) in enumerate(chunks):
        a, b = ab
        if i + 1 < len(chunks):
            ab = up(*chunks[i + 1])
        act = (a * jax.nn.sigmoid(a) * b).astype(BF16)
        part = _dot(act, wdn_ref[c0:c0 + cw, :])
        acc = part if acc is None else acc + part
    return acc


def _ffn_out(x, acc, g_post, gate):
    return x + 0.5 * gate * (_rms(acc) * g_post)


def _sub_rows(n_rows):
    return [pl.ds(r, SUB_ROWS) for r in range(0, n_rows, SUB_ROWS)]


def _log_sigmoid(x):
    return jnp.minimum(x, 0.0) - jnp.log1p(jnp.exp(-jnp.abs(x)))


def _gates_t(h, wg_ref, bg_ref):
    g = _dot_nt(wg_ref[...], h) + bg_ref[...]
    half = N_GATES // 2
    return jnp.concatenate([g[:half], _log_sigmoid(g[half:])], axis=0)


def _mixer_in(x1, g_ref, mod_ref):
    return (_rms(x1) * g_ref[2] * (1.0 + mod_ref[4]) + mod_ref[3]).astype(BF16)


def _ffn_proj_kernel(x_ref, mod_ref, g_ref, wup_ref, wdn_ref, wc_ref, wqvo_ref, wkt_ref, wg_ref, bg_ref,
                     cw_ref, *refs):
    n_cast = (len(refs) - 5) // 2
    cast_in, (x1_ref, conv_ref, qvo_ref, kt_ref, gt_ref), cast_out = refs[:n_cast], refs[n_cast:n_cast + 5], \
        refs[n_cast + 5:]
    for src, dst in zip(cast_in, cast_out):
        dst[...] = src[...].astype(BF16)

    rows = _sub_rows(x_ref.shape[0])
    hs = [_ffn_in(x_ref[r, :], g_ref[0], mod_ref[0], mod_ref[1]) for r in rows]
    accs = [_swiglu(h, wup_ref, wdn_ref) for h in hs]
    h2s = []
    for r, acc in zip(rows, accs):
        x1 = _ffn_out(x_ref[r, :], acc, g_ref[1], mod_ref[2])
        x1_ref[r, :] = x1
        h2s.append(_mixer_in(x1, g_ref, mod_ref))

    for r, h in zip(rows, h2s):
        sub = h.shape[0]
        bg = _dot(h, wc_ref[:, 0:CONV_W])
        cu = _dot(h, wc_ref[:, CONV_W:2 * CONV_W]) * _dot(h, wc_ref[:, 2 * CONV_W:3 * CONV_W])
        col = lax.broadcasted_iota(jnp.int32, (sub, 1), 0) % GRID_W
        prev = jnp.where(col == 0, 0.0, pltpu.roll(cu, 1, 0))
        nxt = jnp.where(col == GRID_W - 1, 0.0, pltpu.roll(cu, sub - 1, 0))
        conv = bg * (cw_ref[0] * prev + cw_ref[1] * cu + cw_ref[2] * nxt)
        conv_ref[r, :] = conv.astype(BF16)

        for j in range(3):
            cols = slice(j * MLSTM_W, (j + 1) * MLSTM_W)
            qvo_ref[r, cols] = _dot(h, wqvo_ref[:, cols]).astype(BF16)
        kt_ref[:, r] = (_dot_nt(wkt_ref[...], h) * (HEAD_DIM ** -0.5)).astype(BF16)
        gt_ref[:, r] = _gates_t(h, wg_ref, bg_ref)


def _ffn_proj(x2d, mod4, g3, wup, wdn, wc, wqvo, wkt, wg, bg, cw3, to_cast, seq, tm):
    n, d = x2d.shape
    tpb = seq // tm
    nb = n // seq
    steps = n // tm
    tile = lambda w: pl.BlockSpec((tm, w), lambda i: (i, 0))
    tile_t = lambda r: pl.BlockSpec((None, r, tm), lambda i: (i // tpb, 0, i % tpb))
    views = [w.reshape(-1, MXU_COLS) for w in to_cast]
    for v in views:
        assert v.shape[0] % (steps * BF16_SUBLANES) == 0, v.shape
    slab = lambda v: pl.BlockSpec((v.shape[0] // steps, MXU_COLS), lambda i: (i, 0))
    outs = pl.pallas_call(
        _ffn_proj_kernel,
        grid=(steps,),
        in_specs=[tile(d),
                  pl.BlockSpec((None, N_MOD, 1, d), lambda i: (i // tpb, 0, 0, 0)),
                  _resident(g3.shape), _resident(wup.shape), _resident(wdn.shape),
                  _resident(wc.shape), _resident(wqvo.shape), _resident(wkt.shape), _resident(wg.shape),
                  _resident(bg.shape), _resident(cw3.shape)] + [slab(v) for v in views],
        out_specs=[tile(d), tile(CONV_W), tile(3 * MLSTM_W), tile_t(MLSTM_W), tile_t(N_GATES)]
        + [slab(v) for v in views],
        out_shape=[jax.ShapeDtypeStruct((n, d), F32),
                   jax.ShapeDtypeStruct((n, CONV_W), BF16),
                   jax.ShapeDtypeStruct((n, 3 * MLSTM_W), BF16),
                   jax.ShapeDtypeStruct((nb, MLSTM_W, seq), BF16),
                   jax.ShapeDtypeStruct((nb, N_GATES, seq), F32)]
        + [jax.ShapeDtypeStruct(v.shape, BF16) for v in views],
        compiler_params=_params(),
        name="ffn1_inproj",
    )(x2d, mod4, g3, wup, wdn, wc, wqvo, wkt, wg, bg, cw3, *views)
    return outs[:5], [o.reshape(w.shape) for o, w in zip(outs[5:], to_cast)]


def _ffn_proj_ctx_kernel(x_ref, mod_ref, g_ref, wup_ref, wdn_ref, wv_ref, wkt_ref, wg_ref, bg_ref,
                         v_ref, kt_ref, gt_ref):
    n_sub, ctx_len = kt_ref.shape[0], kt_ref.shape[2]
    rows = [pl.ds(r * ctx_len, ctx_len) for r in range(n_sub)]
    hs = [_ffn_in(x_ref[r, :], g_ref[0], mod_ref[0], mod_ref[1]) for r in rows]
    accs = [_swiglu(h, wup_ref, wdn_ref) for h in hs]
    h2s = [_mixer_in(_ffn_out(x_ref[r, :], acc, g_ref[1], mod_ref[2]), g_ref, mod_ref)
           for r, acc in zip(rows, accs)]
    for j, (r, h) in enumerate(zip(rows, h2s)):
        v_ref[r, :] = _dot(h, wv_ref[...]).astype(BF16)
        kt_ref[j] = (_dot_nt(wkt_ref[...], h) * (HEAD_DIM ** -0.5)).astype(BF16)
        gt_ref[j] = _gates_t(h, wg_ref, bg_ref)


def _ffn_proj_ctx(c2d, mod4, g3, wup, wdn, wv, wkt, wg, bg, ctx_len, ctx_row):
    n, d = c2d.shape
    assert (n // ctx_len) % CTX_PER_STEP == 0
    tm = CTX_PER_STEP * ctx_len
    tile_t = lambda r: pl.BlockSpec((CTX_PER_STEP, r, ctx_len), lambda i: (i, 0, 0))
    return pl.pallas_call(
        _ffn_proj_ctx_kernel,
        grid=(n // tm,),
        in_specs=[pl.BlockSpec((tm, d), lambda i: (i, 0)),
                  pl.BlockSpec((None, N_MOD, 1, d), lambda i: (ctx_row, 0, 0, 0)),
                  _resident(g3.shape), _resident(wup.shape), _resident(wdn.shape),
                  _resident(wv.shape), _resident(wkt.shape), _resident(wg.shape), _resident(bg.shape)],
        out_specs=[pl.BlockSpec((tm, MLSTM_W), lambda i: (i, 0)), tile_t(MLSTM_W), tile_t(N_GATES)],
        out_shape=[jax.ShapeDtypeStruct((n, MLSTM_W), BF16),
                   jax.ShapeDtypeStruct((n // ctx_len, MLSTM_W, ctx_len), BF16),
                   jax.ShapeDtypeStruct((n // ctx_len, N_GATES, ctx_len), F32)],
        compiler_params=_params(),
        name="ffn1_inproj_ctx",
    )(c2d, mod4, g3, wup, wdn, wv, wkt, wg, bg)


def _lane_scan(x, op, reverse):
    pos = lax.broadcasted_iota(jnp.int32, x.shape, 1)
    d = 1
    while d < CHUNK:
        if reverse:
            x = jnp.where(pos < CHUNK - d, op(x, pltpu.roll(x, CHUNK - d, 1)), x)
        else:
            x = jnp.where(pos >= d, op(x, pltpu.roll(x, d, 1)), x)
        d *= 2
    return x


def _lane_pick(x, lane):
    idx = lax.broadcasted_iota(jnp.int32, x.shape, 1)
    return jnp.sum(jnp.where(idx == lane, x, 0.0), axis=1, keepdims=True)


def _chunk_rows(c):
    if isinstance(c, int):
        return pl.ds(c * CHUNK, CHUNK)
    return pl.ds(pl.multiple_of(c * CHUNK, CHUNK), CHUNK)


def _gate_scan(ig, lf, m, reverse):
    n = ig.shape[0]
    b = _lane_scan(lf, jnp.add, reverse)
    a = ig - b
    b_last = _lane_pick(b, 0 if reverse else CHUNK - 1)
    wmax = b_last + jnp.max(a, axis=1, keepdims=True)
    sub = lax.broadcasted_iota(jnp.int32, (n, 1), 0)
    m_in = jnp.zeros((n, 1), F32)
    m_out = jnp.zeros((n, 1), F32)
    for c in (reversed(range(n)) if reverse else range(n)):
        m_new = jnp.maximum(b_last[c:c + 1] + m, wmax[c:c + 1])
        m_in = jnp.where(sub == c, m, m_in)
        m_out = jnp.where(sub == c, m_new, m_out)
        m = m_new
    big_m = jnp.maximum(m_in, _lane_scan(a, jnp.maximum, reverse))
    e = jnp.exp(-(b + big_m))
    w = jnp.exp(a + b_last - m_out)
    decay = jnp.exp(b_last + m_in - m_out)
    return (a, big_m, e, w, decay, m_in), m


def _v_aug(v):
    return jnp.concatenate([v, jnp.ones((CHUNK, HEAD_DIM), BF16)], axis=1)


def _mlstm_kernel(q_ref, kt_ref, v_ref, g_ref, ckt_ref, cv_ref, cg_ref, out_ref, s_ref, rows_ref, neg_ref):
    nc = kt_ref.shape[1] // CHUNK
    ncc = ckt_ref.shape[1] // CHUNK

    for h in range(N_HEADS):
        hs = slice(h * HEAD_DIM, (h + 1) * HEAD_DIM)
        for d in range(2):
            hd = 2 * h + d
            rev = d == 1
            i_row, f_row = d * N_HEADS + h, (2 + d) * N_HEADS + h
            (_, _, _, w, decay, _), m = _gate_scan(cg_ref[i_row], cg_ref[f_row], jnp.zeros((1, 1), F32), rev)
            s = jnp.zeros((HEAD_DIM, 2 * HEAD_DIM), F32)
            for c in (reversed(range(ncc)) if rev else range(ncc)):
                tok = _chunk_rows(c)
                kw = (ckt_ref[hs, tok].astype(F32) * w[c:c + 1]).astype(BF16)
                s = decay[c:c + 1] * s + _dot(kw, _v_aug(cv_ref[tok, hs]))
            s_ref[hd] = s
            (a, big_m, e, w, decay, m_in), _ = _gate_scan(g_ref[i_row], g_ref[f_row], m, rev)
            decay = jnp.broadcast_to(decay, (nc, CHUNK))
            m_in = jnp.broadcast_to(m_in, (nc, CHUNK))
            vals = ((ROW_A, a * LOG2E), (ROW_M, big_m * LOG2E), (ROW_E, e), (ROW_W, w),
                    (ROW_DECAY, decay), (ROW_MIN, m_in * LOG2E))
            for c in range(nc):
                for r, val in vals:
                    rows_ref[hd, c, r:r + 1, :] = val[c:c + 1]

    row_id = lax.broadcasted_iota(jnp.int32, (CHUNK, CHUNK), 0)
    col_id = lax.broadcasted_iota(jnp.int32, (CHUNK, CHUNK), 1)
    neg_ref[0] = jnp.where(col_id <= row_id, 0.0, -jnp.inf)
    neg_ref[1] = jnp.where(col_id >= row_id, 0.0, -jnp.inf)

    def to_col(row):
        return jnp.sum(jnp.where(row_id == col_id, row, 0.0), axis=1, keepdims=True)

    def scores(h, d, c):
        hs = slice(h * HEAD_DIM, (h + 1) * HEAD_DIM)
        tok = _chunk_rows(c)
        rows = rows_ref[2 * h + d, c]
        m_col = to_col(rows[ROW_M:ROW_M + 1])
        g_col = jnp.exp2(rows[ROW_MIN:ROW_MIN + 1] - m_col)
        dmat = jnp.exp2(rows[ROW_A:ROW_A + 1] - m_col + neg_ref[d])
        q = q_ref[tok, hs]
        kt = kt_ref[hs, tok]
        lhs = jnp.concatenate([(_dot(q, kt) * dmat).astype(BF16),
                               (q.astype(F32) * g_col).astype(BF16)], axis=1)
        kw = (kt.astype(F32) * rows[ROW_W:ROW_W + 1]).astype(BF16)
        return lhs, kw, rows

    def readout(h, d, c, lhs, kw, rows):
        hs = slice(h * HEAD_DIM, (h + 1) * HEAD_DIM)
        v_aug = _v_aug(v_ref[_chunk_rows(c), hs])
        s_old = s_ref[2 * h + d]
        tot = _dot(lhs, jnp.concatenate([v_aug, s_old.astype(BF16)], axis=0))
        dec = rows[ROW_DECAY:ROW_DECAY + 1]
        s_ref[2 * h + d] = jnp.concatenate([dec, dec], axis=1) * s_old + _dot(kw, v_aug)
        return tot

    def normalise(tot, rows):
        e_col = to_col(rows[ROW_E:ROW_E + 1])
        return tot[:, :HEAD_DIM] / jnp.maximum(jnp.abs(tot[:, HEAD_DIM:]), e_col)

    def iteration(i, accumulate):
        scans = [(h, d, (nc - 1 - i) if d else i) for h in range(N_HEADS) for d in range(2)]
        staged = [scores(*s) for s in scans]
        tots = [readout(*s, *st) for s, st in zip(scans, staged)]
        for (h, d, c), tot, st in zip(scans, tots, staged):
            hh = normalise(tot, st[2])
            dst = (_chunk_rows(c), slice(h * HEAD_DIM, (h + 1) * HEAD_DIM))
            out_ref[dst] = out_ref[dst] + hh if accumulate else hh

    def first_half(i, carry):
        iteration(i, False)
        return carry

    def second_half(i, carry):
        iteration(i, True)
        return carry

    lax.fori_loop(0, nc // 2, first_half, 0, unroll=2)
    lax.fori_loop(nc // 2, nc, second_half, 0, unroll=2)


def _mlstm(qvo, kt, gt, cv, ckt, cgt):
    nb, seq, _ = qvo.shape
    ctx_len = cv.shape[1]
    assert seq % (2 * CHUNK) == 0 and ctx_len % CHUNK == 0
    nc, ncc = seq // CHUNK, ctx_len // CHUNK
    qvo_blk = lambda j: pl.BlockSpec((None, seq, MLSTM_W), lambda b: (b, 0, j))
    whole = lambda *s: pl.BlockSpec((None,) + s, lambda b: (b,) + (0,) * len(s))
    return pl.pallas_call(
        _mlstm_kernel,
        grid=(nb,),
        in_specs=[qvo_blk(0), whole(MLSTM_W, seq), qvo_blk(1), whole(N_GATES, nc, CHUNK),
                  whole(MLSTM_W, ctx_len), whole(ctx_len, MLSTM_W), whole(N_GATES, ncc, CHUNK)],
        out_specs=whole(seq, MLSTM_W),
        out_shape=jax.ShapeDtypeStruct((nb, seq, MLSTM_W), F32),
        scratch_shapes=[pltpu.VMEM((2 * N_HEADS, HEAD_DIM, 2 * HEAD_DIM), F32),
                        pltpu.VMEM((2 * N_HEADS, nc, N_ROWS, CHUNK), F32),
                        pltpu.VMEM((2, CHUNK, CHUNK), F32)],
        compiler_params=_params(),
        name="mlstm",
    )(qvo, kt, qvo, gt.reshape(nb, N_GATES, nc, CHUNK), ckt, cv, cgt.reshape(nb, N_GATES, ncc, CHUNK))


def _mlstm_out(h, o, mh_norm):
    hn = jnp.concatenate([_rms(h[:, j * HEAD_DIM:(j + 1) * HEAD_DIM]) for j in range(N_HEADS)], axis=1)
    return (jax.nn.sigmoid(o.astype(F32)) * (hn * mh_norm)).astype(BF16)


def _out_ffn_kernel(x1_ref, conv_ref, h_ref, o_ref, mod_ref, g_ref, mhn_ref, wo_ref, wup_ref, wdn_ref, out_ref):
    rows = _sub_rows(x1_ref.shape[0])
    hms = [_mlstm_out(h_ref[r, :], o_ref[r, :], mhn_ref[...]) for r in rows]
    ys = [_dot(conv_ref[r, :], wo_ref[0:CONV_W, :]) + _dot(hm, wo_ref[CONV_W:, :]) for r, hm in zip(rows, hms)]
    x2s = [x1_ref[r, :] + mod_ref[5] * (_rms(y) * g_ref[3]) for r, y in zip(rows, ys)]
    hs = [_ffn_in(x2, g_ref[4], mod_ref[6], mod_ref[7]) for x2 in x2s]
    accs = [_swiglu(h, wup_ref, wdn_ref) for h in hs]
    for r, x2, acc in zip(rows, x2s, accs):
        out_ref[r, :] = _ffn_out(x2, acc, g_ref[5], mod_ref[8])


def _out_ffn(x1, conv, hsum, qvo, mod4, g3, mhn, wo, wup, wdn, seq, tm):
    n, d = x1.shape
    tpb = seq // tm
    tile = lambda w: pl.BlockSpec((tm, w), lambda i: (i, 0))
    return pl.pallas_call(
        _out_ffn_kernel,
        grid=(n // tm,),
        in_specs=[tile(d), tile(CONV_W), tile(MLSTM_W),
                  pl.BlockSpec((tm, MLSTM_W), lambda i: (i, 2)),
                  pl.BlockSpec((None, N_MOD, 1, d), lambda i: (i // tpb, 0, 0, 0)),
                  _resident(g3.shape), _resident(mhn.shape), _resident(wo.shape), _resident(wup.shape),
                  _resident(wdn.shape)],
        out_specs=tile(d),
        out_shape=jax.ShapeDtypeStruct((n, d), F32),
        compiler_params=_params(),
        name="outproj_ffn2",
    )(x1, conv, hsum, qvo, mod4, g3, mhn, wo, wup, wdn)


def kernel(x, c, ctx, c_ctx, w_mod, b_mod, norm_g, ffn1_up, ffn1_down, ffn2_up, ffn2_down,
           w_in, b_gates, conv_w, mh_norm, w_out):
    nb, seq, d = x.shape
    ctx_len = ctx.shape[1]
    depth = w_mod.shape[0]
    assert depth == 1, "only the single (last) layer configuration is implemented"
    assert nb + 1 <= MOD_ROWS and seq % GRID_W == 0
    tm_in, tm_out = min(512, seq), min(1024, seq)
    assert tm_in % SUB_ROWS == 0 and tm_out % SUB_ROWS == 0 and SUB_ROWS % GRID_W == 0
    conv_cols = 3 * CONV_W

    cvec = jnp.concatenate([c, c_ctx[None], jnp.zeros((MOD_ROWS - nb - 1, d), F32)], axis=0)
    mod4 = _modulation(cvec, w_mod[0], b_mod[0]).reshape(MOD_ROWS, N_MOD, 1, d)

    g3 = norm_g[0].reshape(6, 1, d)
    w1u, w1d = ffn1_up[0].astype(BF16), ffn1_down[0].astype(BF16)
    win = w_in[0]
    seg = lambda j: win[:, conv_cols + j * MLSTM_W:conv_cols + (j + 1) * MLSTM_W]
    wc = win[:, :conv_cols].astype(BF16)
    wqvo = jnp.concatenate([seg(0), seg(2), seg(3)], axis=1).astype(BF16)
    wv = seg(2).astype(BF16)
    wkt = seg(1).T.astype(BF16)
    wg = win[:, conv_cols + 4 * MLSTM_W:].T.astype(BF16)
    bg = b_gates[0].reshape(N_GATES, 1)
    cw3 = conv_w[0].reshape(3, 1, CONV_W)
    mh = mh_norm[0].reshape(1, MLSTM_W)

    (x1, conv, qvo, kt, gt), (wo, w2u, w2d) = _ffn_proj(
        x.reshape(nb * seq, d), mod4, g3, w1u, w1d, wc, wqvo, wkt, wg, bg, cw3,
        (w_out[0], ffn2_up[0], ffn2_down[0]), seq, tm_in)
    cv, ckt, cgt = _ffn_proj_ctx(ctx.reshape(nb * ctx_len, d), mod4, g3, w1u, w1d, wv, wkt, wg, bg,
                                 ctx_len, nb)
    hsum = _mlstm(qvo.reshape(nb, seq, 3 * MLSTM_W), kt, gt, cv.reshape(nb, ctx_len, MLSTM_W), ckt, cgt)
    out = _out_ffn(x1, conv, hsum.reshape(nb * seq, MLSTM_W), qvo, mod4, g3, mh, wo, w2u, w2d, seq, tm_out)
    return out.reshape(nb, seq, d)
```

```python
import jax
import jax.numpy as jnp
from jax import lax
from jax.experimental import pallas as pl
from jax.experimental.pallas import tpu as pltpu

F32 = jnp.float32
BF16 = jnp.bfloat16

GRID_W = 64
CONV_W = 512
N_HEADS = 4
HEAD_DIM = 128
MLSTM_W = N_HEADS * HEAD_DIM
CHUNK = 128
N_MOD = 9
EPS = 1e-6
LOG2E = 1.4426950408889634
N_GATES = 4 * N_HEADS
MOD_ROWS = 24
VMEM_LIMIT = 56 * 1024 * 1024

FFN_CHUNK = 512
BF16_SUBLANES = 16
SUB_ROWS = 256
CTX_PER_STEP = 2

ROW_A, ROW_M, ROW_E, ROW_W, ROW_DECAY, ROW_MIN, N_ROWS = 0, 1, 2, 3, 4, 5, 8


def _dot(a, b):
    return jnp.dot(a, b, preferred_element_type=F32)


def _dot_nt(a, b):
    return lax.dot_general(a, b, (((1,), (1,)), ((), ())), preferred_element_type=F32)


def _rms(x):
    return x * lax.rsqrt(jnp.mean(x * x, axis=-1, keepdims=True) + EPS)


def _params():
    return pltpu.CompilerParams(dimension_semantics=("arbitrary",), vmem_limit_bytes=VMEM_LIMIT)


def _resident(shape):
    nd = len(shape)
    return pl.BlockSpec(shape, lambda *_: (0,) * nd, pipeline_mode=pl.Buffered(1))


def _mod_kernel(c_ref, w_ref, b_ref, o_ref):
    cv = c_ref[...]
    s = (cv * jax.nn.sigmoid(cv)).astype(BF16)
    o_ref[...] = _dot(s, w_ref[...].astype(BF16)) + b_ref[...]


def _modulation(cvec, w_mod, b_mod):
    d = cvec.shape[1]
    n = w_mod.shape[1]
    bn = n // N_MOD
    return pl.pallas_call(
        _mod_kernel,
        grid=(N_MOD,),
        in_specs=[pl.BlockSpec((MOD_ROWS, d), lambda j: (0, 0)),
                  pl.BlockSpec((d, bn), lambda j: (0, j)),
                  pl.BlockSpec((1, bn), lambda j: (0, j))],
        out_specs=pl.BlockSpec((MOD_ROWS, bn), lambda j: (0, j)),
        out_shape=jax.ShapeDtypeStruct((MOD_ROWS, n), F32),
        compiler_params=_params(),
        name="modulation",
    )(cvec, w_mod, b_mod.reshape(1, n))


def _ffn_in(x, g_pre, shift, scale):
    return (_rms(x) * g_pre * (1.0 + scale) + shift).astype(BF16)


def _swiglu(h, wup_ref, wdn_ref):
    ffn = wdn_ref.shape[0]
    chunks = [(c0, min(FFN_CHUNK, ffn - c0)) for c0 in range(0, ffn, FFN_CHUNK)]

    def up(c0, cw):
        return _dot(h, wup_ref[:, c0:c0 + cw]), _dot(h, wup_ref[:, ffn + c0:ffn + c0 + cw])

    acc = None
    ab = up(*chunks[0])
    for i, (c0, cw) in enumerate(chunks):
        a, b = ab
        if i + 1 < len(chunks):
            ab = up(*chunks[i + 1])
        act = (a * jax.nn.sigmoid(a) * b).astype(BF16)
        part = _dot(act, wdn_ref[c0:c0 + cw, :])
        acc = part if acc is None else acc + part
    return acc


def _ffn_out(x, acc, g_post, gate):
    return x + 0.5 * gate * (_rms(acc) * g_post)


def _sub_rows(n_rows):
    return [pl.ds(r, SUB_ROWS) for r in range(0, n_rows, SUB_ROWS)]


def _log_sigmoid(x):
    return jnp.minimum(x, 0.0) - jnp.log1p(jnp.exp(-jnp.abs(x)))


def _gates_t(h, wg_ref, bg_ref):
    g = _dot_nt(wg_ref[...], h) + bg_ref[...]
    half = N_GATES // 2
    return jnp.concatenate([g[:half], _log_sigmoid(g[half:])], axis=0)


def _mixer_in(x1, g_ref, mod_ref):
    return (_rms(x1) * g_ref[2] * (1.0 + mod_ref[4]) + mod_ref[3]).astype(BF16)


def _ffn_proj_kernel(x_ref, mod_ref, g_ref, wup_ref, wdn_ref, wc_ref, wqvo_ref, wkt_ref, wg_ref, bg_ref,
                     cw_ref, *refs):
    n_cast = (len(refs) - 5) // 2
    cast_in, (x1_ref, conv_ref, qvo_ref, kt_ref, gt_ref), cast_out = refs[:n_cast], refs[n_cast:n_cast + 5], \
        refs[n_cast + 5:]
    for src, dst in zip(cast_in, cast_out):
        dst[...] = src[...].astype(BF16)

    rows = _sub_rows(x_ref.shape[0])
    hs = [_ffn_in(x_ref[r, :], g_ref[0], mod_ref[0], mod_ref[1]) for r in rows]
    accs = [_swiglu(h, wup_ref, wdn_ref) for h in hs]
    h2s = []
    for r, acc in zip(rows, accs):
        x1 = _ffn_out(x_ref[r, :], acc, g_ref[1], mod_ref[2])
        x1_ref[r, :] = x1
        h2s.append(_mixer_in(x1, g_ref, mod_ref))

    for r, h in zip(rows, h2s):
        sub = h.shape[0]
        bg = _dot(h, wc_ref[:, 0:CONV_W])
        cu = _dot(h, wc_ref[:, CONV_W:2 * CONV_W]) * _dot(h, wc_ref[:, 2 * CONV_W:3 * CONV_W])
        col = lax.broadcasted_iota(jnp.int32, (sub, 1), 0) % GRID_W
        prev = jnp.where(col == 0, 0.0, pltpu.roll(cu, 1, 0))
        nxt = jnp.where(col == GRID_W - 1, 0.0, pltpu.roll(cu, sub - 1, 0))
        conv = bg * (cw_ref[0] * prev + cw_ref[1] * cu + cw_ref[2] * nxt)
        conv_ref[r, :] = conv.astype(BF16)

        for j in range(3):
            cols = slice(j * MLSTM_W, (j + 1) * MLSTM_W)
            qvo_ref[r, cols] = _dot(h, wqvo_ref[:, cols]).astype(BF16)
        kt_ref[:, r] = (_dot_nt(wkt_ref[...], h) * (HEAD_DIM ** -0.5)).astype(BF16)
        gt_ref[:, r] = _gates_t(h, wg_ref, bg_ref)


def _ffn_proj(x2d, mod4, g3, wup, wdn, wc, wqvo, wkt, wg, bg, cw3, to_cast, seq, tm):
    n, d = x2d.shape
    tpb = seq // tm
    nb = n // seq
    steps = n // tm
    tile = lambda w: pl.BlockSpec((tm, w), lambda i: (i, 0))
    tile_t = lambda r: pl.BlockSpec((None, r, tm), lambda i: (i // tpb, 0, i % tpb))
    def slab_specs(w):
        rows = next(r for r in range(BF16_SUBLANES, w.shape[1] + 1, BF16_SUBLANES)
                    if w.shape[1] % r == 0 and w.shape[1] // r <= steps)
        last = w.shape[1] // rows - 1
        return (pl.BlockSpec((None, rows, w.shape[2]), lambda i: (0, jnp.minimum(i, last), 0)),
                pl.BlockSpec((rows, w.shape[2]), lambda i: (jnp.minimum(i, last), 0)))
    cast_specs = [slab_specs(w) for w in to_cast]
    outs = pl.pallas_call(
        _ffn_proj_kernel,
        grid=(steps,),
        in_specs=[tile(d),
                  pl.BlockSpec((None, N_MOD, 1, d), lambda i: (i // tpb, 0, 0, 0)),
                  _resident(g3.shape), _resident(wup.shape), _resident(wdn.shape),
                  _resident(wc.shape), _resident(wqvo.shape), _resident(wkt.shape), _resident(wg.shape),
                  _resident(bg.shape), _resident(cw3.shape)] + [spec[0] for spec in cast_specs],
        out_specs=[tile(d), tile(CONV_W), tile(3 * MLSTM_W), tile_t(MLSTM_W), tile_t(N_GATES)]
        + [spec[1] for spec in cast_specs],
        out_shape=[jax.ShapeDtypeStruct((n, d), F32),
                   jax.ShapeDtypeStruct((n, CONV_W), BF16),
                   jax.ShapeDtypeStruct((n, 3 * MLSTM_W), BF16),
                   jax.ShapeDtypeStruct((nb, MLSTM_W, seq), BF16),
                   jax.ShapeDtypeStruct((nb, N_GATES, seq), F32)]
        + [jax.ShapeDtypeStruct(w.shape[1:], BF16) for w in to_cast],
        compiler_params=_params(),
        name="ffn1_inproj",
    )(x2d, mod4, g3, wup, wdn, wc, wqvo, wkt, wg, bg, cw3, *to_cast)
    return outs[:5], outs[5:]


def _ffn_proj_ctx_kernel(x_ref, mod_ref, g_ref, wup_ref, wdn_ref, wv_ref, wkt_ref, wg_ref, bg_ref,
                         v_ref, kt_ref, gt_ref):
    n_sub, ctx_len = kt_ref.shape[0], kt_ref.shape[2]
    rows = [pl.ds(r * ctx_len, ctx_len) for r in range(n_sub)]
    hs = [_ffn_in(x_ref[r, :], g_ref[0], mod_ref[0], mod_ref[1]) for r in rows]
    accs = [_swiglu(h, wup_ref, wdn_ref) for h in hs]
    h2s = [_mixer_in(_ffn_out(x_ref[r, :], acc, g_ref[1], mod_ref[2]), g_ref, mod_ref)
           for r, acc in zip(rows, accs)]
    for j, (r, h) in enumerate(zip(rows, h2s)):
        v_ref[r, :] = _dot(h, wv_ref[...]).astype(BF16)
        kt_ref[j] = (_dot_nt(wkt_ref[...], h) * (HEAD_DIM ** -0.5)).astype(BF16)
        gt_ref[j] = _gates_t(h, wg_ref, bg_ref)


def _ffn_proj_ctx(c2d, mod4, g3, wup, wdn, wv, wkt, wg, bg, ctx_len, ctx_row):
    n, d = c2d.shape
    assert (n // ctx_len) % CTX_PER_STEP == 0
    tm = CTX_PER_STEP * ctx_len
    tile_t = lambda r: pl.BlockSpec((CTX_PER_STEP, r, ctx_len), lambda i: (i, 0, 0))
    return pl.pallas_call(
        _ffn_proj_ctx_kernel,
        grid=(n // tm,),
        in_specs=[pl.BlockSpec((tm, d), lambda i: (i, 0)),
                  pl.BlockSpec((None, N_MOD, 1, d), lambda i: (ctx_row, 0, 0, 0)),
                  _resident(g3.shape), _resident(wup.shape), _resident(wdn.shape),
                  _resident(wv.shape), _resident(wkt.shape), _resident(wg.shape), _resident(bg.shape)],
        out_specs=[pl.BlockSpec((tm, MLSTM_W), lambda i: (i, 0)), tile_t(MLSTM_W), tile_t(N_GATES)],
        out_shape=[jax.ShapeDtypeStruct((n, MLSTM_W), BF16),
                   jax.ShapeDtypeStruct((n // ctx_len, MLSTM_W, ctx_len), BF16),
                   jax.ShapeDtypeStruct((n // ctx_len, N_GATES, ctx_len), F32)],
        compiler_params=_params(),
        name="ffn1_inproj_ctx",
    )(c2d, mod4, g3, wup, wdn, wv, wkt, wg, bg)


def _lane_scan(x, op, reverse):
    pos = lax.broadcasted_iota(jnp.int32, x.shape, 1)
    d = 1
    while d < CHUNK:
        if reverse:
            x = jnp.where(pos < CHUNK - d, op(x, pltpu.roll(x, CHUNK - d, 1)), x)
        else:
            x = jnp.where(pos >= d, op(x, pltpu.roll(x, d, 1)), x)
        d *= 2
    return x


def _lane_pick(x, lane):
    idx = lax.broadcasted_iota(jnp.int32, x.shape, 1)
    return jnp.sum(jnp.where(idx == lane, x, 0.0), axis=1, keepdims=True)


def _chunk_rows(c):
    if isinstance(c, int):
        return pl.ds(c * CHUNK, CHUNK)
    return pl.ds(pl.multiple_of(c * CHUNK, CHUNK), CHUNK)


def _gate_scan(ig, lf, m, reverse):
    n = ig.shape[0]
    b = _lane_scan(lf, jnp.add, reverse)
    a = ig - b
    b_last = _lane_pick(b, 0 if reverse else CHUNK - 1)
    wmax = b_last + jnp.max(a, axis=1, keepdims=True)
    sub = lax.broadcasted_iota(jnp.int32, (n, 1), 0)
    m_in = jnp.zeros((n, 1), F32)
    m_out = jnp.zeros((n, 1), F32)
    for c in (reversed(range(n)) if reverse else range(n)):
        m_new = jnp.maximum(b_last[c:c + 1] + m, wmax[c:c + 1])
        m_in = jnp.where(sub == c, m, m_in)
        m_out = jnp.where(sub == c, m_new, m_out)
        m = m_new
    big_m = jnp.maximum(m_in, _lane_scan(a, jnp.maximum, reverse))
    e = jnp.exp(-(b + big_m))
    w = jnp.exp(a + b_last - m_out)
    decay = jnp.exp(b_last + m_in - m_out)
    return (a, big_m, e, w, decay, m_in), m


def _v_aug(v):
    return jnp.concatenate([v, jnp.ones((CHUNK, HEAD_DIM), BF16)], axis=1)


def _mlstm_kernel(q_ref, kt_ref, v_ref, g_ref, ckt_ref, cv_ref, cg_ref, out_ref, s_ref, rows_ref, neg_ref):
    nc = kt_ref.shape[1] // CHUNK
    ncc = ckt_ref.shape[1] // CHUNK

    for h in range(N_HEADS):
        hs = slice(h * HEAD_DIM, (h + 1) * HEAD_DIM)
        for d in range(2):
            hd = 2 * h + d
            rev = d == 1
            i_row, f_row = d * N_HEADS + h, (2 + d) * N_HEADS + h
            (_, _, _, w, decay, _), m = _gate_scan(cg_ref[i_row], cg_ref[f_row], jnp.zeros((1, 1), F32), rev)
            s = jnp.zeros((HEAD_DIM, 2 * HEAD_DIM), F32)
            for c in (reversed(range(ncc)) if rev else range(ncc)):
                tok = _chunk_rows(c)
                kw = (ckt_ref[hs, tok].astype(F32) * w[c:c + 1]).astype(BF16)
                s = decay[c:c + 1] * s + _dot(kw, _v_aug(cv_ref[tok, hs]))
            s_ref[hd] = s
            (a, big_m, e, w, decay, m_in), _ = _gate_scan(g_ref[i_row], g_ref[f_row], m, rev)
            decay = jnp.broadcast_to(decay, (nc, CHUNK))
            m_in = jnp.broadcast_to(m_in, (nc, CHUNK))
            vals = ((ROW_A, a * LOG2E), (ROW_M, big_m * LOG2E), (ROW_E, e), (ROW_W, w),
                    (ROW_DECAY, decay), (ROW_MIN, m_in * LOG2E))
            for c in range(nc):
                for r, val in vals:
                    rows_ref[hd, c, r:r + 1, :] = val[c:c + 1]

    row_id = lax.broadcasted_iota(jnp.int32, (CHUNK, CHUNK), 0)
    col_id = lax.broadcasted_iota(jnp.int32, (CHUNK, CHUNK), 1)
    neg_ref[0] = jnp.where(col_id <= row_id, 0.0, -jnp.inf)
    neg_ref[1] = jnp.where(col_id >= row_id, 0.0, -jnp.inf)

    def to_col(row):
        return jnp.sum(jnp.where(row_id == col_id, row, 0.0), axis=1, keepdims=True)

    def scores(h, d, c):
        hs = slice(h * HEAD_DIM, (h + 1) * HEAD_DIM)
        tok = _chunk_rows(c)
        rows = rows_ref[2 * h + d, c]
        m_col = to_col(rows[ROW_M:ROW_M + 1])
        g_col = jnp.exp2(rows[ROW_MIN:ROW_MIN + 1] - m_col)
        dmat = jnp.exp2(rows[ROW_A:ROW_A + 1] - m_col + neg_ref[d])
        q = q_ref[tok, hs]
        kt = kt_ref[hs, tok]
        lhs = jnp.concatenate([(_dot(q, kt) * dmat).astype(BF16),
                               (q.astype(F32) * g_col).astype(BF16)], axis=1)
        kw = (kt.astype(F32) * rows[ROW_W:ROW_W + 1]).astype(BF16)
        return lhs, kw, rows

    def readout(h, d, c, lhs, kw, rows):
        hs = slice(h * HEAD_DIM, (h + 1) * HEAD_DIM)
        v_aug = _v_aug(v_ref[_chunk_rows(c), hs])
        s_old = s_ref[2 * h + d]
        tot = _dot(lhs, jnp.concatenate([v_aug, s_old.astype(BF16)], axis=0))
        dec = rows[ROW_DECAY:ROW_DECAY + 1]
        s_ref[2 * h + d] = jnp.concatenate([dec, dec], axis=1) * s_old + _dot(kw, v_aug)
        return tot

    def normalise(tot, rows):
        e_col = to_col(rows[ROW_E:ROW_E + 1])
        return tot[:, :HEAD_DIM] / jnp.maximum(jnp.abs(tot[:, HEAD_DIM:]), e_col)

    def iteration(i, accumulate):
        scans = [(h, d, (nc - 1 - i) if d else i) for h in range(N_HEADS) for d in range(2)]
        staged = [scores(*s) for s in scans]
        tots = [readout(*s, *st) for s, st in zip(scans, staged)]
        for (h, d, c), tot, st in zip(scans, tots, staged):
            hh = normalise(tot, st[2])
            dst = (_chunk_rows(c), slice(h * HEAD_DIM, (h + 1) * HEAD_DIM))
            out_ref[dst] = out_ref[dst] + hh if accumulate else hh

    def first_half(i, carry):
        iteration(i, False)
        return carry

    def second_half(i, carry):
        iteration(i, True)
        return carry

    lax.fori_loop(0, nc // 2, first_half, 0, unroll=2)
    lax.fori_loop(nc // 2, nc, second_half, 0, unroll=2)


def _mlstm(qvo, kt, gt, cv, ckt, cgt):
    nb, seq, _ = qvo.shape
    ctx_len = cv.shape[1]
    assert seq % (2 * CHUNK) == 0 and ctx_len % CHUNK == 0
    nc, ncc = seq // CHUNK, ctx_len // CHUNK
    qvo_blk = lambda j: pl.BlockSpec((None, seq, MLSTM_W), lambda b: (b, 0, j))
    whole = lambda *s: pl.BlockSpec((None,) + s, lambda b: (b,) + (0,) * len(s))
    return pl.pallas_call(
        _mlstm_kernel,
        grid=(nb,),
        in_specs=[qvo_blk(0), whole(MLSTM_W, seq), qvo_blk(1), whole(N_GATES, nc, CHUNK),
                  whole(MLSTM_W, ctx_len), whole(ctx_len, MLSTM_W), whole(N_GATES, ncc, CHUNK)],
        out_specs=whole(seq, MLSTM_W),
        out_shape=jax.ShapeDtypeStruct((nb, seq, MLSTM_W), F32),
        scratch_shapes=[pltpu.VMEM((2 * N_HEADS, HEAD_DIM, 2 * HEAD_DIM), F32),
                        pltpu.VMEM((2 * N_HEADS, nc, N_ROWS, CHUNK), F32),
                        pltpu.VMEM((2, CHUNK, CHUNK), F32)],
        compiler_params=_params(),
        name="mlstm",
    )(qvo, kt, qvo, gt.reshape(nb, N_GATES, nc, CHUNK), ckt, cv, cgt.reshape(nb, N_GATES, ncc, CHUNK))


def _mlstm_out(h, o, mh_norm):
    hn = jnp.concatenate([_rms(h[:, j * HEAD_DIM:(j + 1) * HEAD_DIM]) for j in range(N_HEADS)], axis=1)
    return (jax.nn.sigmoid(o.astype(F32)) * (hn * mh_norm)).astype(BF16)


def _out_ffn_kernel(x1_ref, conv_ref, h_ref, o_ref, mod_ref, g_ref, mhn_ref, wo_ref, wup_ref, wdn_ref, out_ref):
    rows = _sub_rows(x1_ref.shape[0])
    hms = [_mlstm_out(h_ref[r, :], o_ref[r, :], mhn_ref[...]) for r in rows]
    ys = [_dot(conv_ref[r, :], wo_ref[0:CONV_W, :]) + _dot(hm, wo_ref[CONV_W:, :]) for r, hm in zip(rows, hms)]
    x2s = [x1_ref[r, :] + mod_ref[5] * (_rms(y) * g_ref[3]) for r, y in zip(rows, ys)]
    hs = [_ffn_in(x2, g_ref[4], mod_ref[6], mod_ref[7]) for x2 in x2s]
    accs = [_swiglu(h, wup_ref, wdn_ref) for h in hs]
    for r, x2, acc in zip(rows, x2s, accs):
        out_ref[r, :] = _ffn_out(x2, acc, g_ref[5], mod_ref[8])


def _out_ffn(x1, conv, hsum, qvo, mod4, g3, mhn, wo, wup, wdn, seq, tm):
    n, d = x1.shape
    tpb = seq // tm
    tile = lambda w: pl.BlockSpec((tm, w), lambda i: (i, 0))
    return pl.pallas_call(
        _out_ffn_kernel,
        grid=(n // tm,),
        in_specs=[tile(d), tile(CONV_W), tile(MLSTM_W),
                  pl.BlockSpec((tm, MLSTM_W), lambda i: (i, 2)),
                  pl.BlockSpec((None, N_MOD, 1, d), lambda i: (i // tpb, 0, 0, 0)),
                  _resident(g3.shape), _resident(mhn.shape), _resident(wo.shape), _resident(wup.shape),
                  _resident(wdn.shape)],
        out_specs=tile(d),
        out_shape=jax.ShapeDtypeStruct((n, d), F32),
        compiler_params=_params(),
        name="outproj_ffn2",
    )(x1, conv, hsum, qvo, mod4, g3, mhn, wo, wup, wdn)


def kernel(x, c, ctx, c_ctx, w_mod, b_mod, norm_g, ffn1_up, ffn1_down, ffn2_up, ffn2_down,
           w_in, b_gates, conv_w, mh_norm, w_out):
    nb, seq, d = x.shape
    ctx_len = ctx.shape[1]
    depth = w_mod.shape[0]
    assert depth == 1, "only the single (last) layer configuration is implemented"
    assert nb + 1 <= MOD_ROWS and seq % GRID_W == 0
    tm_in, tm_out = min(512, seq), min(1024, seq)
    assert tm_in % SUB_ROWS == 0 and tm_out % SUB_ROWS == 0 and SUB_ROWS % GRID_W == 0
    conv_cols = 3 * CONV_W

    cvec = jnp.concatenate([c, c_ctx[None], jnp.zeros((MOD_ROWS - nb - 1, d), F32)], axis=0)
    mod4 = _modulation(cvec, w_mod[0], b_mod[0]).reshape(MOD_ROWS, N_MOD, 1, d)

    g3 = norm_g[0].reshape(6, 1, d)
    w1u, w1d = ffn1_up[0].astype(BF16), ffn1_down[0].astype(BF16)
    win = w_in[0]
    seg = lambda j: win[:, conv_cols + j * MLSTM_W:conv_cols + (j + 1) * MLSTM_W]
    wc = win[:, :conv_cols].astype(BF16)
    wqvo = jnp.concatenate([seg(0), seg(2), seg(3)], axis=1).astype(BF16)
    wv = seg(2).astype(BF16)
    wkt = seg(1).T.astype(BF16)
    wg = win[:, conv_cols + 4 * MLSTM_W:].T.astype(BF16)
    bg = b_gates[0].reshape(N_GATES, 1)
    cw3 = conv_w[0].reshape(3, 1, CONV_W)
    mh = mh_norm[0].reshape(1, MLSTM_W)

    (x1, conv, qvo, kt, gt), (wo, w2u, w2d) = _ffn_proj(
        x.reshape(nb * seq, d), mod4, g3, w1u, w1d, wc, wqvo, wkt, wg, bg, cw3,
        (w_out, ffn2_up, ffn2_down), seq, tm_in)
    cv, ckt, cgt = _ffn_proj_ctx(ctx.reshape(nb * ctx_len, d), mod4, g3, w1u, w1d, wv, wkt, wg, bg,
                                 ctx_len, nb)
    hsum = _mlstm(qvo.reshape(nb, seq, 3 * MLSTM_W), kt, gt, cv.reshape(nb, ctx_len, MLSTM_W), ckt, cgt)
    out = _out_ffn(x1, conv, hsum.reshape(nb * seq, MLSTM_W), qvo, mod4, g3, mh, wo, w2u, w2d, seq, tm_out)
    return out.reshape(nb, seq, d)
```

```python
import jax
import jax.numpy as jnp
from jax import lax
from jax.experimental import pallas as pl
from jax.experimental.pallas import tpu as pltpu

F32 = jnp.float32
BF16 = jnp.bfloat16

GRID_W = 64
CONV_W = 512
N_HEADS = 4
HEAD_DIM = 128
MLSTM_W = N_HEADS * HEAD_DIM
CHUNK = 128
N_MOD = 9
EPS = 1e-6
LOG2E = 1.4426950408889634
N_GATES = 4 * N_HEADS
MOD_ROWS = 24
VMEM_LIMIT = 56 * 1024 * 1024

FFN_CHUNK = 512
BF16_SUBLANES = 16
SUB_ROWS = 256
CTX_PER_STEP = 2

ROW_A, ROW_M, ROW_E, ROW_W, ROW_DECAY, ROW_MIN, N_ROWS = 0, 1, 2, 3, 4, 5, 8


def _dot(a, b):
    return jnp.dot(a, b, preferred_element_type=F32)


def _dot_nt(a, b):
    return lax.dot_general(a, b, (((1,), (1,)), ((), ())), preferred_element_type=F32)


def _rms(x):
    return x * lax.rsqrt(jnp.mean(x * x, axis=-1, keepdims=True) + EPS)


def _params():
    return pltpu.CompilerParams(dimension_semantics=("arbitrary",), vmem_limit_bytes=VMEM_LIMIT)


def _resident(shape):
    nd = len(shape)
    return pl.BlockSpec(shape, lambda *_: (0,) * nd, pipeline_mode=pl.Buffered(1))


def _mod_kernel(c_ref, w_ref, b_ref, o_ref):
    cv = c_ref[...]
    s = (cv * jax.nn.sigmoid(cv)).astype(BF16)
    o_ref[...] = _dot(s, w_ref[...].astype(BF16)) + b_ref[...]


def _modulation(cvec, w_mod, b_mod):
    d = cvec.shape[1]
    n = w_mod.shape[1]
    bn = n // N_MOD
    return pl.pallas_call(
        _mod_kernel,
        grid=(N_MOD,),
        in_specs=[pl.BlockSpec((MOD_ROWS, d), lambda j: (0, 0)),
                  pl.BlockSpec((d, bn), lambda j: (0, j)),
                  pl.BlockSpec((1, bn), lambda j: (0, j))],
        out_specs=pl.BlockSpec((MOD_ROWS, bn), lambda j: (0, j)),
        out_shape=jax.ShapeDtypeStruct((MOD_ROWS, n), F32),
        compiler_params=_params(),
        name="modulation",
    )(cvec, w_mod, b_mod.reshape(1, n))


def _ffn_in(x, g_pre, shift, scale):
    return (_rms(x) * g_pre * (1.0 + scale) + shift).astype(BF16)


def _swiglu(h, wup_ref, wdn_ref):
    ffn = wdn_ref.shape[0]
    chunks = [(c0, min(FFN_CHUNK, ffn - c0)) for c0 in range(0, ffn, FFN_CHUNK)]

    def up(c0, cw):
        return _dot(h, wup_ref[:, c0:c0 + cw]), _dot(h, wup_ref[:, ffn + c0:ffn + c0 + cw])

    acc = None
    ab = up(*chunks[0])
    for i, (c0, cw) in enumerate(chunks):
        a, b = ab
        if i + 1 < len(chunks):
            ab = up(*chunks[i + 1])
        act = (a * jax.nn.sigmoid(a) * b).astype(BF16)
        part = _dot(act, wdn_ref[c0:c0 + cw, :])
        acc = part if acc is None else acc + part
    return acc


def _ffn_out(x, acc, g_post, gate):
    return x + 0.5 * gate * (_rms(acc) * g_post)


def _sub_rows(n_rows):
    return [pl.ds(r, SUB_ROWS) for r in range(0, n_rows, SUB_ROWS)]


def _log_sigmoid(x):
    return jnp.minimum(x, 0.0) - jnp.log1p(jnp.exp(-jnp.abs(x)))


def _gates_t(h, wg_ref, bg_ref):
    g = _dot_nt(wg_ref[...], h) + bg_ref[...]
    half = N_GATES // 2
    return jnp.concatenate([g[:half], _log_sigmoid(g[half:])], axis=0)


def _mixer_in(x1, g_ref, mod_ref):
    return (_rms(x1) * g_ref[2] * (1.0 + mod_ref[4]) + mod_ref[3]).astype(BF16)


def _ffn_proj_kernel(x_ref, mod_ref, g_ref, wup_ref, wdn_ref, wc_ref, wqvo_ref, wkt_ref, wg_ref, bg_ref,
                     cw_ref, *refs):
    n_cast = (len(refs) - 5) // 2
    cast_in, (x1_ref, conv_ref, qvo_ref, kt_ref, gt_ref), cast_out = refs[:n_cast], refs[n_cast:n_cast + 5], \
        refs[n_cast + 5:]
    for src, dst in zip(cast_in, cast_out):
        dst[...] = src[...].astype(BF16)

    rows = _sub_rows(x_ref.shape[0])
    hs = [_ffn_in(x_ref[r, :], g_ref[0], mod_ref[0], mod_ref[1]) for r in rows]
    accs = [_swiglu(h, wup_ref, wdn_ref) for h in hs]
    h2s = []
    for r, acc in zip(rows, accs):
        x1 = _ffn_out(x_ref[r, :], acc, g_ref[1], mod_ref[2])
        x1_ref[r, :] = x1
        h2s.append(_mixer_in(x1, g_ref, mod_ref))

    for r, h in zip(rows, h2s):
        sub = h.shape[0]
        bg = _dot(h, wc_ref[:, 0:CONV_W])
        cu = _dot(h, wc_ref[:, CONV_W:2 * CONV_W]) * _dot(h, wc_ref[:, 2 * CONV_W:3 * CONV_W])
        col = lax.broadcasted_iota(jnp.int32, (sub, 1), 0) % GRID_W
        prev = jnp.where(col == 0, 0.0, pltpu.roll(cu, 1, 0))
        nxt = jnp.where(col == GRID_W - 1, 0.0, pltpu.roll(cu, sub - 1, 0))
        conv = bg * (cw_ref[0] * prev + cw_ref[1] * cu + cw_ref[2] * nxt)
        conv_ref[r, :] = conv.astype(BF16)

        for j in range(3):
            cols = slice(j * MLSTM_W, (j + 1) * MLSTM_W)
            qvo_ref[r, cols] = _dot(h, wqvo_ref[:, cols]).astype(BF16)
        kt_ref[:, r] = (_dot_nt(wkt_ref[...], h) * (HEAD_DIM ** -0.5)).astype(BF16)
        gt_ref[:, r] = _gates_t(h, wg_ref, bg_ref)


def _ffn_proj(x2d, mod4, g3, wup, wdn, wc, wqvo, wkt, wg, bg, cw3, to_cast, seq, tm):
    n, d = x2d.shape
    tpb = seq // tm
    nb = n // seq
    steps = n // tm
    tile = lambda w: pl.BlockSpec((tm, w), lambda i: (i, 0))
    tile_t = lambda r: pl.BlockSpec((None, r, tm), lambda i: (i // tpb, 0, i % tpb))
    def slab_specs(w):
        rows = next(r for r in range(BF16_SUBLANES, w.shape[1] + 1, BF16_SUBLANES)
                    if w.shape[1] % r == 0 and w.shape[1] // r <= steps)
        last = w.shape[1] // rows - 1
        return (pl.BlockSpec((None, rows, w.shape[2]), lambda i: (0, jnp.minimum(i, last), 0)),
                pl.BlockSpec((rows, w.shape[2]), lambda i: (jnp.minimum(i, last), 0)))
    cast_specs = [slab_specs(w) for w in to_cast]
    outs = pl.pallas_call(
        _ffn_proj_kernel,
        grid=(steps,),
        in_specs=[tile(d),
                  pl.BlockSpec((None, N_MOD, 1, d), lambda i: (i // tpb, 0, 0, 0)),
                  _resident(g3.shape), _resident(wup.shape), _resident(wdn.shape),
                  _resident(wc.shape), _resident(wqvo.shape), _resident(wkt.shape), _resident(wg.shape),
                  _resident(bg.shape), _resident(cw3.shape)] + [spec[0] for spec in cast_specs],
        out_specs=[tile(d), tile(CONV_W), tile(3 * MLSTM_W), tile_t(MLSTM_W), tile_t(N_GATES)]
        + [spec[1] for spec in cast_specs],
        out_shape=[jax.ShapeDtypeStruct((n, d), F32),
                   jax.ShapeDtypeStruct((n, CONV_W), BF16),
                   jax.ShapeDtypeStruct((n, 3 * MLSTM_W), BF16),
                   jax.ShapeDtypeStruct((nb, MLSTM_W, seq), BF16),
                   jax.ShapeDtypeStruct((nb, N_GATES, seq), F32)]
        + [jax.ShapeDtypeStruct(w.shape[1:], BF16) for w in to_cast],
        compiler_params=_params(),
        name="ffn1_inproj",
    )(x2d, mod4, g3, wup, wdn, wc, wqvo, wkt, wg, bg, cw3, *to_cast)
    return outs[:5], outs[5:]


def _swiglu_f32w(hs, wup_ref, wdn_ref):
    ffn = wdn_ref.shape[0]
    chunks = [(c0, min(FFN_CHUNK, ffn - c0)) for c0 in range(0, ffn, FFN_CHUNK)]

    def up(c0, cw):
        wa = wup_ref[:, c0:c0 + cw].astype(BF16)
        wb = wup_ref[:, ffn + c0:ffn + c0 + cw].astype(BF16)
        return [(_dot(h, wa), _dot(h, wb)) for h in hs]

    accs = [None] * len(hs)
    abs_ = up(*chunks[0])
    for i, (c0, cw) in enumerate(chunks):
        cur = abs_
        if i + 1 < len(chunks):
            abs_ = up(*chunks[i + 1])
        wd = wdn_ref[c0:c0 + cw, :].astype(BF16)
        for j, (a, b) in enumerate(cur):
            part = _dot((a * jax.nn.sigmoid(a) * b).astype(BF16), wd)
            accs[j] = part if accs[j] is None else accs[j] + part
    return accs


def _ffn_proj_ctx_kernel(x_ref, mod_ref, g_ref, wup_ref, wdn_ref, wv_ref, wkt_ref, wg_ref, bg_ref,
                         v_ref, kt_ref, gt_ref, wup_bf_ref, wdn_bf_ref):
    step = pl.program_id(0)
    for src, dst in ((wup_ref, wup_bf_ref), (wdn_ref, wdn_bf_ref)):
        slab = dst.shape[0]
        dst[...] = src[pl.ds(pl.multiple_of(step * slab, slab), slab), :].astype(BF16)

    n_sub, ctx_len = kt_ref.shape[0], kt_ref.shape[2]
    rows = [pl.ds(r * ctx_len, ctx_len) for r in range(n_sub)]
    hs = [_ffn_in(x_ref[r, :], g_ref[0], mod_ref[0], mod_ref[1]) for r in rows]
    accs = _swiglu_f32w(hs, wup_ref, wdn_ref)
    h2s = [_mixer_in(_ffn_out(x_ref[r, :], acc, g_ref[1], mod_ref[2]), g_ref, mod_ref)
           for r, acc in zip(rows, accs)]
    for j, (r, h) in enumerate(zip(rows, h2s)):
        v_ref[r, :] = _dot(h, wv_ref[...]).astype(BF16)
        kt_ref[j] = (_dot_nt(wkt_ref[...], h) * (HEAD_DIM ** -0.5)).astype(BF16)
        gt_ref[j] = _gates_t(h, wg_ref, bg_ref)


def _ffn_proj_ctx(c2d, mod4, g3, wup_f32, wdn_f32, wv, wkt, wg, bg, ctx_len, ctx_row):
    n, d = c2d.shape
    assert (n // ctx_len) % CTX_PER_STEP == 0
    tm = CTX_PER_STEP * ctx_len
    steps = n // tm
    for w in (wup_f32, wdn_f32):
        assert w.shape[1] % (steps * BF16_SUBLANES) == 0, (w.shape, steps)
    tile_t = lambda r: pl.BlockSpec((CTX_PER_STEP, r, ctx_len), lambda i: (i, 0, 0))
    layer0 = lambda w: pl.BlockSpec((None,) + w.shape[1:], lambda i: (0, 0, 0), pipeline_mode=pl.Buffered(1))
    slab = lambda w: pl.BlockSpec((w.shape[1] // steps, w.shape[2]), lambda i: (i, 0))
    return pl.pallas_call(
        _ffn_proj_ctx_kernel,
        grid=(steps,),
        in_specs=[pl.BlockSpec((tm, d), lambda i: (i, 0)),
                  pl.BlockSpec((None, N_MOD, 1, d), lambda i: (ctx_row, 0, 0, 0)),
                  _resident(g3.shape), layer0(wup_f32), layer0(wdn_f32),
                  _resident(wv.shape), _resident(wkt.shape), _resident(wg.shape), _resident(bg.shape)],
        out_specs=[pl.BlockSpec((tm, MLSTM_W), lambda i: (i, 0)), tile_t(MLSTM_W), tile_t(N_GATES),
                   slab(wup_f32), slab(wdn_f32)],
        out_shape=[jax.ShapeDtypeStruct((n, MLSTM_W), BF16),
                   jax.ShapeDtypeStruct((n // ctx_len, MLSTM_W, ctx_len), BF16),
                   jax.ShapeDtypeStruct((n // ctx_len, N_GATES, ctx_len), F32),
                   jax.ShapeDtypeStruct(wup_f32.shape[1:], BF16),
                   jax.ShapeDtypeStruct(wdn_f32.shape[1:], BF16)],
        compiler_params=_params(),
        name="ffn1_inproj_ctx",
    )(c2d, mod4, g3, wup_f32, wdn_f32, wv, wkt, wg, bg)


def _lane_scan(x, op, reverse):
    pos = lax.broadcasted_iota(jnp.int32, x.shape, 1)
    d = 1
    while d < CHUNK:
        if reverse:
            x = jnp.where(pos < CHUNK - d, op(x, pltpu.roll(x, CHUNK - d, 1)), x)
        else:
            x = jnp.where(pos >= d, op(x, pltpu.roll(x, d, 1)), x)
        d *= 2
    return x


def _lane_pick(x, lane):
    idx = lax.broadcasted_iota(jnp.int32, x.shape, 1)
    return jnp.sum(jnp.where(idx == lane, x, 0.0), axis=1, keepdims=True)


def _chunk_rows(c):
    if isinstance(c, int):
        return pl.ds(c * CHUNK, CHUNK)
    return pl.ds(pl.multiple_of(c * CHUNK, CHUNK), CHUNK)


def _gate_scan(ig, lf, m, reverse):
    n = ig.shape[0]
    b = _lane_scan(lf, jnp.add, reverse)
    a = ig - b
    b_last = _lane_pick(b, 0 if reverse else CHUNK - 1)
    wmax = b_last + jnp.max(a, axis=1, keepdims=True)
    sub = lax.broadcasted_iota(jnp.int32, (n, 1), 0)
    m_in = jnp.zeros((n, 1), F32)
    m_out = jnp.zeros((n, 1), F32)
    for c in (reversed(range(n)) if reverse else range(n)):
        m_new = jnp.maximum(b_last[c:c + 1] + m, wmax[c:c + 1])
        m_in = jnp.where(sub == c, m, m_in)
        m_out = jnp.where(sub == c, m_new, m_out)
        m = m_new
    big_m = jnp.maximum(m_in, _lane_scan(a, jnp.maximum, reverse))
    e = jnp.exp(-(b + big_m))
    w = jnp.exp(a + b_last - m_out)
    decay = jnp.exp(b_last + m_in - m_out)
    return (a, big_m, e, w, decay, m_in), m


def _v_aug(v):
    return jnp.concatenate([v, jnp.ones((CHUNK, HEAD_DIM), BF16)], axis=1)


def _mlstm_kernel(q_ref, kt_ref, v_ref, g_ref, ckt_ref, cv_ref, cg_ref, out_ref, s_ref, rows_ref, neg_ref):
    nc = kt_ref.shape[1] // CHUNK
    ncc = ckt_ref.shape[1] // CHUNK

    for h in range(N_HEADS):
        hs = slice(h * HEAD_DIM, (h + 1) * HEAD_DIM)
        for d in range(2):
            hd = 2 * h + d
            rev = d == 1
            i_row, f_row = d * N_HEADS + h, (2 + d) * N_HEADS + h
            (_, _, _, w, decay, _), m = _gate_scan(cg_ref[i_row], cg_ref[f_row], jnp.zeros((1, 1), F32), rev)
            s = jnp.zeros((HEAD_DIM, 2 * HEAD_DIM), F32)
            for c in (reversed(range(ncc)) if rev else range(ncc)):
                tok = _chunk_rows(c)
                kw = (ckt_ref[hs, tok].astype(F32) * w[c:c + 1]).astype(BF16)
                s = decay[c:c + 1] * s + _dot(kw, _v_aug(cv_ref[tok, hs]))
            s_ref[hd] = s
            (a, big_m, e, w, decay, m_in), _ = _gate_scan(g_ref[i_row], g_ref[f_row], m, rev)
            decay = jnp.broadcast_to(decay, (nc, CHUNK))
            m_in = jnp.broadcast_to(m_in, (nc, CHUNK))
            vals = ((ROW_A, a * LOG2E), (ROW_M, big_m * LOG2E), (ROW_E, e), (ROW_W, w),
                    (ROW_DECAY, decay), (ROW_MIN, m_in * LOG2E))
            for c in range(nc):
                for r, val in vals:
                    rows_ref[hd, c, r:r + 1, :] = val[c:c + 1]

    row_id = lax.broadcasted_iota(jnp.int32, (CHUNK, CHUNK), 0)
    col_id = lax.broadcasted_iota(jnp.int32, (CHUNK, CHUNK), 1)
    neg_ref[0] = jnp.where(col_id <= row_id, 0.0, -jnp.inf)
    neg_ref[1] = jnp.where(col_id >= row_id, 0.0, -jnp.inf)

    def to_col(row):
        return jnp.sum(jnp.where(row_id == col_id, row, 0.0), axis=1, keepdims=True)

    def scores(h, d, c):
        hs = slice(h * HEAD_DIM, (h + 1) * HEAD_DIM)
        tok = _chunk_rows(c)
        rows = rows_ref[2 * h + d, c]
        m_col = to_col(rows[ROW_M:ROW_M + 1])
        g_col = jnp.exp2(rows[ROW_MIN:ROW_MIN + 1] - m_col)
        dmat = jnp.exp2(rows[ROW_A:ROW_A + 1] - m_col + neg_ref[d])
        q = q_ref[tok, hs]
        kt = kt_ref[hs, tok]
        lhs = jnp.concatenate([(_dot(q, kt) * dmat).astype(BF16),
                               (q.astype(F32) * g_col).astype(BF16)], axis=1)
        kw = (kt.astype(F32) * rows[ROW_W:ROW_W + 1]).astype(BF16)
        return lhs, kw, rows

    def readout(h, d, c, lhs, kw, rows):
        hs = slice(h * HEAD_DIM, (h + 1) * HEAD_DIM)
        v_aug = _v_aug(v_ref[_chunk_rows(c), hs])
        s_old = s_ref[2 * h + d]
        tot = _dot(lhs, jnp.concatenate([v_aug, s_old.astype(BF16)], axis=0))
        dec = rows[ROW_DECAY:ROW_DECAY + 1]
        s_ref[2 * h + d] = jnp.concatenate([dec, dec], axis=1) * s_old + _dot(kw, v_aug)
        return tot

    def normalise(tot, rows):
        e_col = to_col(rows[ROW_E:ROW_E + 1])
        return tot[:, :HEAD_DIM] / jnp.maximum(jnp.abs(tot[:, HEAD_DIM:]), e_col)

    def iteration(i, accumulate):
        scans = [(h, d, (nc - 1 - i) if d else i) for h in range(N_HEADS) for d in range(2)]
        staged = [scores(*s) for s in scans]
        tots = [readout(*s, *st) for s, st in zip(scans, staged)]
        for (h, d, c), tot, st in zip(scans, tots, staged):
            hh = normalise(tot, st[2])
            dst = (_chunk_rows(c), slice(h * HEAD_DIM, (h + 1) * HEAD_DIM))
            out_ref[dst] = out_ref[dst] + hh if accumulate else hh

    def first_half(i, carry):
        iteration(i, False)
        return carry

    def second_half(i, carry):
        iteration(i, True)
        return carry

    lax.fori_loop(0, nc // 2, first_half, 0, unroll=2)
    lax.fori_loop(nc // 2, nc, second_half, 0, unroll=2)


def _mlstm(qvo, kt, gt, cv, ckt, cgt):
    nb, seq, _ = qvo.shape
    ctx_len = cv.shape[1]
    assert seq % (2 * CHUNK) == 0 and ctx_len % CHUNK == 0
    nc, ncc = seq // CHUNK, ctx_len // CHUNK
    qvo_blk = lambda j: pl.BlockSpec((None, seq, MLSTM_W), lambda b: (b, 0, j))
    whole = lambda *s: pl.BlockSpec((None,) + s, lambda b: (b,) + (0,) * len(s))
    return pl.pallas_call(
        _mlstm_kernel,
        grid=(nb,),
        in_specs=[qvo_blk(0), whole(MLSTM_W, seq), qvo_blk(1), whole(N_GATES, nc, CHUNK),
                  whole(MLSTM_W, ctx_len), whole(ctx_len, MLSTM_W), whole(N_GATES, ncc, CHUNK)],
        out_specs=whole(seq, MLSTM_W),
        out_shape=jax.ShapeDtypeStruct((nb, seq, MLSTM_W), F32),
        scratch_shapes=[pltpu.VMEM((2 * N_HEADS, HEAD_DIM, 2 * HEAD_DIM), F32),
                        pltpu.VMEM((2 * N_HEADS, nc, N_ROWS, CHUNK), F32),
                        pltpu.VMEM((2, CHUNK, CHUNK), F32)],
        compiler_params=_params(),
        name="mlstm",
    )(qvo, kt, qvo, gt.reshape(nb, N_GATES, nc, CHUNK), ckt, cv, cgt.reshape(nb, N_GATES, ncc, CHUNK))


def _mlstm_out(h, o, mh_norm):
    hn = jnp.concatenate([_rms(h[:, j * HEAD_DIM:(j + 1) * HEAD_DIM]) for j in range(N_HEADS)], axis=1)
    return (jax.nn.sigmoid(o.astype(F32)) * (hn * mh_norm)).astype(BF16)


def _out_ffn_kernel(x1_ref, conv_ref, h_ref, o_ref, mod_ref, g_ref, mhn_ref, wo_ref, wup_ref, wdn_ref, out_ref):
    rows = _sub_rows(x1_ref.shape[0])
    hms = [_mlstm_out(h_ref[r, :], o_ref[r, :], mhn_ref[...]) for r in rows]
    ys = [_dot(conv_ref[r, :], wo_ref[0:CONV_W, :]) + _dot(hm, wo_ref[CONV_W:, :]) for r, hm in zip(rows, hms)]
    x2s = [x1_ref[r, :] + mod_ref[5] * (_rms(y) * g_ref[3]) for r, y in zip(rows, ys)]
    hs = [_ffn_in(x2, g_ref[4], mod_ref[6], mod_ref[7]) for x2 in x2s]
    accs = [_swiglu(h, wup_ref, wdn_ref) for h in hs]
    for r, x2, acc in zip(rows, x2s, accs):
        out_ref[r, :] = _ffn_out(x2, acc, g_ref[5], mod_ref[8])


def _out_ffn(x1, conv, hsum, qvo, mod4, g3, mhn, wo, wup, wdn, seq, tm):
    n, d = x1.shape
    tpb = seq // tm
    tile = lambda w: pl.BlockSpec((tm, w), lambda i: (i, 0))
    return pl.pallas_call(
        _out_ffn_kernel,
        grid=(n // tm,),
        in_specs=[tile(d), tile(CONV_W), tile(MLSTM_W),
                  pl.BlockSpec((tm, MLSTM_W), lambda i: (i, 2)),
                  pl.BlockSpec((None, N_MOD, 1, d), lambda i: (i // tpb, 0, 0, 0)),
                  _resident(g3.shape), _resident(mhn.shape), _resident(wo.shape), _resident(wup.shape),
                  _resident(wdn.shape)],
        out_specs=tile(d),
        out_shape=jax.ShapeDtypeStruct((n, d), F32),
        compiler_params=_params(),
        name="outproj_ffn2",
    )(x1, conv, hsum, qvo, mod4, g3, mhn, wo, wup, wdn)


def kernel(x, c, ctx, c_ctx, w_mod, b_mod, norm_g, ffn1_up, ffn1_down, ffn2_up, ffn2_down,
           w_in, b_gates, conv_w, mh_norm, w_out):
    nb, seq, d = x.shape
    ctx_len = ctx.shape[1]
    depth = w_mod.shape[0]
    assert depth == 1, "only the single (last) layer configuration is implemented"
    assert nb + 1 <= MOD_ROWS and seq % GRID_W == 0
    tm_in, tm_out = min(512, seq), min(1024, seq)
    assert tm_in % SUB_ROWS == 0 and tm_out % SUB_ROWS == 0 and SUB_ROWS % GRID_W == 0
    conv_cols = 3 * CONV_W

    cvec = jnp.concatenate([c, c_ctx[None], jnp.zeros((MOD_ROWS - nb - 1, d), F32)], axis=0)
    mod4 = _modulation(cvec, w_mod[0], b_mod[0]).reshape(MOD_ROWS, N_MOD, 1, d)

    g3 = norm_g[0].reshape(6, 1, d)
    win = w_in[0]
    seg = lambda j: win[:, conv_cols + j * MLSTM_W:conv_cols + (j + 1) * MLSTM_W]
    wc = win[:, :conv_cols].astype(BF16)
    wqvo = jnp.concatenate([seg(0), seg(2), seg(3)], axis=1).astype(BF16)
    wv = seg(2).astype(BF16)
    wkt = seg(1).T.astype(BF16)
    wg = win[:, conv_cols + 4 * MLSTM_W:].T.astype(BF16)
    bg = b_gates[0].reshape(N_GATES, 1)
    cw3 = conv_w[0].reshape(3, 1, CONV_W)
    mh = mh_norm[0].reshape(1, MLSTM_W)

    cv, ckt, cgt, w1u, w1d = _ffn_proj_ctx(ctx.reshape(nb * ctx_len, d), mod4, g3, ffn1_up, ffn1_down, wv, wkt, wg,
                                           bg, ctx_len, nb)
    (x1, conv, qvo, kt, gt), (wo, w2u, w2d) = _ffn_proj(
        x.reshape(nb * seq, d), mod4, g3, w1u, w1d, wc, wqvo, wkt, wg, bg, cw3,
        (w_out, ffn2_up, ffn2_down), seq, tm_in)
    hsum = _mlstm(qvo.reshape(nb, seq, 3 * MLSTM_W), kt, gt, cv.reshape(nb, ctx_len, MLSTM_W), ckt, cgt)
    out = _out_ffn(x1, conv, hsum.reshape(nb * seq, MLSTM_W), qvo, mod4, g3, mh, wo, w2u, w2d, seq, tm_out)
    return out.reshape(nb, seq, d)
```

```python
import jax
import jax.numpy as jnp
from jax import lax
from jax.experimental import pallas as pl
from jax.experimental.pallas import tpu as pltpu

F32 = jnp.float32
BF16 = jnp.bfloat16

GRID_W = 64
CONV_W = 512
N_HEADS = 4
HEAD_DIM = 128
MLSTM_W = N_HEADS * HEAD_DIM
CHUNK = 128
N_MOD = 9
EPS = 1e-6
LOG2E = 1.4426950408889634
N_GATES = 4 * N_HEADS
MOD_ROWS = 24
VMEM_LIMIT = 56 * 1024 * 1024

FFN_CHUNK = 512
BF16_SUBLANES = 16
SUB_ROWS = 256
CTX_PER_STEP = 2

ROW_A, ROW_M, ROW_E, ROW_W, ROW_DECAY, ROW_MIN, N_ROWS = 0, 1, 2, 3, 4, 5, 8


def _dot(a, b):
    return jnp.dot(a, b, preferred_element_type=F32)


def _dot_nt(a, b):
    return lax.dot_general(a, b, (((1,), (1,)), ((), ())), preferred_element_type=F32)


def _rms(x):
    return x * lax.rsqrt(jnp.mean(x * x, axis=-1, keepdims=True) + EPS)


def _params():
    return pltpu.CompilerParams(dimension_semantics=("arbitrary",), vmem_limit_bytes=VMEM_LIMIT)


def _resident(shape):
    nd = len(shape)
    return pl.BlockSpec(shape, lambda *_: (0,) * nd, pipeline_mode=pl.Buffered(1))


def _mod_kernel(c_ref, w_ref, b_ref, o_ref):
    cv = c_ref[...]
    s = (cv * jax.nn.sigmoid(cv)).astype(BF16)
    o_ref[...] = _dot(s, w_ref[...].astype(BF16)) + b_ref[...]


def _modulation(cvec, w_mod, b_mod):
    d = cvec.shape[1]
    n = w_mod.shape[1]
    bn = n // N_MOD
    return pl.pallas_call(
        _mod_kernel,
        grid=(N_MOD,),
        in_specs=[pl.BlockSpec((MOD_ROWS, d), lambda j: (0, 0)),
                  pl.BlockSpec((d, bn), lambda j: (0, j)),
                  pl.BlockSpec((1, bn), lambda j: (0, j))],
        out_specs=pl.BlockSpec((MOD_ROWS, bn), lambda j: (0, j)),
        out_shape=jax.ShapeDtypeStruct((MOD_ROWS, n), F32),
        compiler_params=_params(),
        name="modulation",
    )(cvec, w_mod, b_mod.reshape(1, n))


def _ffn_in(x, g_pre, shift, scale):
    return (_rms(x) * g_pre * (1.0 + scale) + shift).astype(BF16)


def _swiglu(h, wup_ref, wdn_ref):
    ffn = wdn_ref.shape[0]
    chunks = [(c0, min(FFN_CHUNK, ffn - c0)) for c0 in range(0, ffn, FFN_CHUNK)]

    def up(c0, cw):
        return _dot(h, wup_ref[:, c0:c0 + cw]), _dot(h, wup_ref[:, ffn + c0:ffn + c0 + cw])

    acc = None
    ab = up(*chunks[0])
    for i, (c0, cw) in enumerate(chunks):
        a, b = ab
        if i + 1 < len(chunks):
            ab = up(*chunks[i + 1])
        act = (a * jax.nn.sigmoid(a) * b).astype(BF16)
        part = _dot(act, wdn_ref[c0:c0 + cw, :])
        acc = part if acc is None else acc + part
    return acc


def _ffn_out(x, acc, g_post, gate):
    return x + 0.5 * gate * (_rms(acc) * g_post)


def _sub_rows(n_rows):
    return [pl.ds(r, SUB_ROWS) for r in range(0, n_rows, SUB_ROWS)]


def _log_sigmoid(x):
    return jnp.minimum(x, 0.0) - jnp.log1p(jnp.exp(-jnp.abs(x)))


def _gates_t(h, wg_ref, bg_ref):
    g = _dot_nt(wg_ref[...], h) + bg_ref[...]
    half = N_GATES // 2
    return jnp.concatenate([g[:half], _log_sigmoid(g[half:])], axis=0)


def _mixer_in(x1, g_ref, mod_ref):
    return (_rms(x1) * g_ref[2] * (1.0 + mod_ref[4]) + mod_ref[3]).astype(BF16)


def _ffn_proj_kernel(x_ref, mod_ref, g_ref, wup_ref, wdn_ref, wc_ref, wqvo_ref, wkt_ref, wg_ref, bg_ref,
                     cw_ref, *refs):
    n_cast = (len(refs) - 7) // 2
    cast_in, cast_out = refs[:n_cast], refs[n_cast + 7:]
    x1_ref, conv_ref, q_ref, v_ref, o_ref, kt_ref, gt_ref = refs[n_cast:n_cast + 7]
    for src, dst in zip(cast_in, cast_out):
        dst[...] = src[...].astype(BF16)

    rows = _sub_rows(x_ref.shape[0])
    hs = [_ffn_in(x_ref[r, :], g_ref[0], mod_ref[0], mod_ref[1]) for r in rows]
    accs = [_swiglu(h, wup_ref, wdn_ref) for h in hs]
    h2s = []
    for r, acc in zip(rows, accs):
        x1 = _ffn_out(x_ref[r, :], acc, g_ref[1], mod_ref[2])
        x1_ref[r, :] = x1
        h2s.append(_mixer_in(x1, g_ref, mod_ref))

    for r, h in zip(rows, h2s):
        sub = h.shape[0]
        bg = _dot(h, wc_ref[:, 0:CONV_W])
        cu = _dot(h, wc_ref[:, CONV_W:2 * CONV_W]) * _dot(h, wc_ref[:, 2 * CONV_W:3 * CONV_W])
        col = lax.broadcasted_iota(jnp.int32, (sub, 1), 0) % GRID_W
        prev = jnp.where(col == 0, 0.0, pltpu.roll(cu, 1, 0))
        nxt = jnp.where(col == GRID_W - 1, 0.0, pltpu.roll(cu, sub - 1, 0))
        conv = bg * (cw_ref[0] * prev + cw_ref[1] * cu + cw_ref[2] * nxt)
        conv_ref[r, :] = conv.astype(BF16)

        for j, dst in enumerate((q_ref, v_ref, o_ref)):
            dst[r, :] = _dot(h, wqvo_ref[:, j * MLSTM_W:(j + 1) * MLSTM_W]).astype(BF16)
        kt_ref[:, r] = (_dot_nt(wkt_ref[...], h) * (HEAD_DIM ** -0.5)).astype(BF16)
        gt_ref[:, r] = _gates_t(h, wg_ref, bg_ref)


def _ffn_proj(x2d, mod4, g3, wup, wdn, wc, wqvo, wkt, wg, bg, cw3, to_cast, seq, tm):
    n, d = x2d.shape
    tpb = seq // tm
    nb = n // seq
    steps = n // tm
    tile = lambda w: pl.BlockSpec((tm, w), lambda i: (i, 0))
    tile_t = lambda r: pl.BlockSpec((None, r, tm), lambda i: (i // tpb, 0, i % tpb))
    def slab_specs(w):
        rows = next(r for r in range(BF16_SUBLANES, w.shape[1] + 1, BF16_SUBLANES)
                    if w.shape[1] % r == 0 and w.shape[1] // r <= steps)
        last = w.shape[1] // rows - 1
        return (pl.BlockSpec((None, rows, w.shape[2]), lambda i: (0, jnp.minimum(i, last), 0)),
                pl.BlockSpec((rows, w.shape[2]), lambda i: (jnp.minimum(i, last), 0)))
    cast_specs = [slab_specs(w) for w in to_cast]
    outs = pl.pallas_call(
        _ffn_proj_kernel,
        grid=(steps,),
        in_specs=[tile(d),
                  pl.BlockSpec((None, N_MOD, 1, d), lambda i: (i // tpb, 0, 0, 0)),
                  _resident(g3.shape), _resident(wup.shape), _resident(wdn.shape),
                  _resident(wc.shape), _resident(wqvo.shape), _resident(wkt.shape), _resident(wg.shape),
                  _resident(bg.shape), _resident(cw3.shape)] + [spec[0] for spec in cast_specs],
        out_specs=[tile(d), tile(CONV_W), tile(MLSTM_W), tile(MLSTM_W), tile(MLSTM_W), tile_t(MLSTM_W), tile_t(N_GATES)]
        + [spec[1] for spec in cast_specs],
        out_shape=[jax.ShapeDtypeStruct((n, d), F32),
                   jax.ShapeDtypeStruct((n, CONV_W), BF16)] + [jax.ShapeDtypeStruct((n, MLSTM_W), BF16)] * 3
        + [jax.ShapeDtypeStruct((nb, MLSTM_W, seq), BF16),
                   jax.ShapeDtypeStruct((nb, N_GATES, seq), F32)]
        + [jax.ShapeDtypeStruct(w.shape[1:], BF16) for w in to_cast],
        compiler_params=_params(),
        name="ffn1_inproj",
    )(x2d, mod4, g3, wup, wdn, wc, wqvo, wkt, wg, bg, cw3, *to_cast)
    return outs[:7], outs[7:]


def _ffn_proj_ctx_kernel(x_ref, mod_ref, g_ref, wup_ref, wdn_ref, wv_ref, wkt_ref, wg_ref, bg_ref,
                         v_ref, kt_ref, gt_ref):
    n_sub, ctx_len = kt_ref.shape[0], kt_ref.shape[2]
    rows = [pl.ds(r * ctx_len, ctx_len) for r in range(n_sub)]
    hs = [_ffn_in(x_ref[r, :], g_ref[0], mod_ref[0], mod_ref[1]) for r in rows]
    accs = [_swiglu(h, wup_ref, wdn_ref) for h in hs]
    h2s = [_mixer_in(_ffn_out(x_ref[r, :], acc, g_ref[1], mod_ref[2]), g_ref, mod_ref)
           for r, acc in zip(rows, accs)]
    for j, (r, h) in enumerate(zip(rows, h2s)):
        v_ref[r, :] = _dot(h, wv_ref[...]).astype(BF16)
        kt_ref[j] = (_dot_nt(wkt_ref[...], h) * (HEAD_DIM ** -0.5)).astype(BF16)
        gt_ref[j] = _gates_t(h, wg_ref, bg_ref)


def _ffn_proj_ctx(c2d, mod4, g3, wup, wdn, wv, wkt, wg, bg, ctx_len, ctx_row):
    n, d = c2d.shape
    assert (n // ctx_len) % CTX_PER_STEP == 0
    tm = CTX_PER_STEP * ctx_len
    tile_t = lambda r: pl.BlockSpec((CTX_PER_STEP, r, ctx_len), lambda i: (i, 0, 0))
    return pl.pallas_call(
        _ffn_proj_ctx_kernel,
        grid=(n // tm,),
        in_specs=[pl.BlockSpec((tm, d), lambda i: (i, 0)),
                  pl.BlockSpec((None, N_MOD, 1, d), lambda i: (ctx_row, 0, 0, 0)),
                  _resident(g3.shape), _resident(wup.shape), _resident(wdn.shape),
                  _resident(wv.shape), _resident(wkt.shape), _resident(wg.shape), _resident(bg.shape)],
        out_specs=[pl.BlockSpec((tm, MLSTM_W), lambda i: (i, 0)), tile_t(MLSTM_W), tile_t(N_GATES)],
        out_shape=[jax.ShapeDtypeStruct((n, MLSTM_W), BF16),
                   jax.ShapeDtypeStruct((n // ctx_len, MLSTM_W, ctx_len), BF16),
                   jax.ShapeDtypeStruct((n // ctx_len, N_GATES, ctx_len), F32)],
        compiler_params=_params(),
        name="ffn1_inproj_ctx",
    )(c2d, mod4, g3, wup, wdn, wv, wkt, wg, bg)


def _lane_scan(x, op, reverse):
    pos = lax.broadcasted_iota(jnp.int32, x.shape, 1)
    d = 1
    while d < CHUNK:
        if reverse:
            x = jnp.where(pos < CHUNK - d, op(x, pltpu.roll(x, CHUNK - d, 1)), x)
        else:
            x = jnp.where(pos >= d, op(x, pltpu.roll(x, d, 1)), x)
        d *= 2
    return x


def _lane_pick(x, lane):
    idx = lax.broadcasted_iota(jnp.int32, x.shape, 1)
    return jnp.sum(jnp.where(idx == lane, x, 0.0), axis=1, keepdims=True)


def _chunk_rows(c):
    if isinstance(c, int):
        return pl.ds(c * CHUNK, CHUNK)
    return pl.ds(pl.multiple_of(c * CHUNK, CHUNK), CHUNK)


def _gate_scan(ig, lf, m, reverse):
    n = ig.shape[0]
    b = _lane_scan(lf, jnp.add, reverse)
    a = ig - b
    b_last = _lane_pick(b, 0 if reverse else CHUNK - 1)
    wmax = b_last + jnp.max(a, axis=1, keepdims=True)
    sub = lax.broadcasted_iota(jnp.int32, (n, 1), 0)
    m_in = jnp.zeros((n, 1), F32)
    m_out = jnp.zeros((n, 1), F32)
    for c in (reversed(range(n)) if reverse else range(n)):
        m_new = jnp.maximum(b_last[c:c + 1] + m, wmax[c:c + 1])
        m_in = jnp.where(sub == c, m, m_in)
        m_out = jnp.where(sub == c, m_new, m_out)
        m = m_new
    big_m = jnp.maximum(m_in, _lane_scan(a, jnp.maximum, reverse))
    e = jnp.exp(-(b + big_m))
    w = jnp.exp(a + b_last - m_out)
    decay = jnp.exp(b_last + m_in - m_out)
    return (a, big_m, e, w, decay, m_in), m


def _v_aug(v):
    return jnp.concatenate([v, jnp.ones((CHUNK, HEAD_DIM), BF16)], axis=1)


def _mlstm_kernel(q_ref, kt_ref, v_ref, g_ref, ckt_ref, cv_ref, cg_ref, out_ref, s_ref, rows_ref, neg_ref):
    nc = kt_ref.shape[1] // CHUNK
    ncc = ckt_ref.shape[1] // CHUNK

    for h in range(N_HEADS):
        hs = slice(h * HEAD_DIM, (h + 1) * HEAD_DIM)
        for d in range(2):
            hd = 2 * h + d
            rev = d == 1
            i_row, f_row = d * N_HEADS + h, (2 + d) * N_HEADS + h
            (_, _, _, w, decay, _), m = _gate_scan(cg_ref[i_row], cg_ref[f_row], jnp.zeros((1, 1), F32), rev)
            s = jnp.zeros((HEAD_DIM, 2 * HEAD_DIM), F32)
            for c in (reversed(range(ncc)) if rev else range(ncc)):
                tok = _chunk_rows(c)
                kw = (ckt_ref[hs, tok].astype(F32) * w[c:c + 1]).astype(BF16)
                s = decay[c:c + 1] * s + _dot(kw, _v_aug(cv_ref[tok, hs]))
            s_ref[hd] = s
            (a, big_m, e, w, decay, m_in), _ = _gate_scan(g_ref[i_row], g_ref[f_row], m, rev)
            decay = jnp.broadcast_to(decay, (nc, CHUNK))
            m_in = jnp.broadcast_to(m_in, (nc, CHUNK))
            vals = ((ROW_A, a * LOG2E), (ROW_M, big_m * LOG2E), (ROW_E, e), (ROW_W, w),
                    (ROW_DECAY, decay), (ROW_MIN, m_in * LOG2E))
            for c in range(nc):
                for r, val in vals:
                    rows_ref[hd, c, r:r + 1, :] = val[c:c + 1]

    row_id = lax.broadcasted_iota(jnp.int32, (CHUNK, CHUNK), 0)
    col_id = lax.broadcasted_iota(jnp.int32, (CHUNK, CHUNK), 1)
    neg_ref[0] = jnp.where(col_id <= row_id, 0.0, -jnp.inf)
    neg_ref[1] = jnp.where(col_id >= row_id, 0.0, -jnp.inf)

    def to_col(row):
        return jnp.sum(jnp.where(row_id == col_id, row, 0.0), axis=1, keepdims=True)

    def scores(h, d, c):
        hs = slice(h * HEAD_DIM, (h + 1) * HEAD_DIM)
        tok = _chunk_rows(c)
        rows = rows_ref[2 * h + d, c]
        m_col = to_col(rows[ROW_M:ROW_M + 1])
        g_col = jnp.exp2(rows[ROW_MIN:ROW_MIN + 1] - m_col)
        dmat = jnp.exp2(rows[ROW_A:ROW_A + 1] - m_col + neg_ref[d])
        q = q_ref[tok, hs]
        kt = kt_ref[hs, tok]
        lhs = jnp.concatenate([(_dot(q, kt) * dmat).astype(BF16),
                               (q.astype(F32) * g_col).astype(BF16)], axis=1)
        kw = (kt.astype(F32) * rows[ROW_W:ROW_W + 1]).astype(BF16)
        return lhs, kw, rows

    def readout(h, d, c, lhs, kw, rows):
        hs = slice(h * HEAD_DIM, (h + 1) * HEAD_DIM)
        v_aug = _v_aug(v_ref[_chunk_rows(c), hs])
        s_old = s_ref[2 * h + d]
        tot = _dot(lhs, jnp.concatenate([v_aug, s_old.astype(BF16)], axis=0))
        dec = rows[ROW_DECAY:ROW_DECAY + 1]
        s_ref[2 * h + d] = jnp.concatenate([dec, dec], axis=1) * s_old + _dot(kw, v_aug)
        return tot

    def normalise(tot, rows):
        e_col = to_col(rows[ROW_E:ROW_E + 1])
        return tot[:, :HEAD_DIM] / jnp.maximum(jnp.abs(tot[:, HEAD_DIM:]), e_col)

    def iteration(i, accumulate):
        scans = [(h, d, (nc - 1 - i) if d else i) for h in range(N_HEADS) for d in range(2)]
        staged = [scores(*s) for s in scans]
        tots = [readout(*s, *st) for s, st in zip(scans, staged)]
        for (h, d, c), tot, st in zip(scans, tots, staged):
            hh = normalise(tot, st[2])
            dst = (_chunk_rows(c), slice(h * HEAD_DIM, (h + 1) * HEAD_DIM))
            out_ref[dst] = out_ref[dst] + hh if accumulate else hh

    def first_half(i, carry):
        iteration(i, False)
        return carry

    def second_half(i, carry):
        iteration(i, True)
        return carry

    lax.fori_loop(0, nc // 2, first_half, 0, unroll=2)
    lax.fori_loop(nc // 2, nc, second_half, 0, unroll=2)


def _mlstm(q, v, kt, gt, cv, ckt, cgt):
    nb, seq, _ = q.shape
    ctx_len = cv.shape[1]
    assert seq % (2 * CHUNK) == 0 and ctx_len % CHUNK == 0
    nc, ncc = seq // CHUNK, ctx_len // CHUNK
    whole = lambda *s: pl.BlockSpec((None,) + s, lambda b: (b,) + (0,) * len(s))
    return pl.pallas_call(
        _mlstm_kernel,
        grid=(nb,),
        in_specs=[whole(seq, MLSTM_W), whole(MLSTM_W, seq), whole(seq, MLSTM_W), whole(N_GATES, nc, CHUNK),
                  whole(MLSTM_W, ctx_len), whole(ctx_len, MLSTM_W), whole(N_GATES, ncc, CHUNK)],
        out_specs=whole(seq, MLSTM_W),
        out_shape=jax.ShapeDtypeStruct((nb, seq, MLSTM_W), F32),
        scratch_shapes=[pltpu.VMEM((2 * N_HEADS, HEAD_DIM, 2 * HEAD_DIM), F32),
                        pltpu.VMEM((2 * N_HEADS, nc, N_ROWS, CHUNK), F32),
                        pltpu.VMEM((2, CHUNK, CHUNK), F32)],
        compiler_params=_params(),
        name="mlstm",
    )(q, kt, v, gt.reshape(nb, N_GATES, nc, CHUNK), ckt, cv, cgt.reshape(nb, N_GATES, ncc, CHUNK))


def _mlstm_out(h, o, mh_norm):
    hn = jnp.concatenate([_rms(h[:, j * HEAD_DIM:(j + 1) * HEAD_DIM]) for j in range(N_HEADS)], axis=1)
    return (jax.nn.sigmoid(o.astype(F32)) * (hn * mh_norm)).astype(BF16)


def _out_ffn_kernel(x1_ref, conv_ref, h_ref, o_ref, mod_ref, g_ref, mhn_ref, wo_ref, wup_ref, wdn_ref, out_ref):
    rows = _sub_rows(x1_ref.shape[0])
    hms = [_mlstm_out(h_ref[r, :], o_ref[r, :], mhn_ref[...]) for r in rows]
    ys = [_dot(conv_ref[r, :], wo_ref[0:CONV_W, :]) + _dot(hm, wo_ref[CONV_W:, :]) for r, hm in zip(rows, hms)]
    x2s = [x1_ref[r, :] + mod_ref[5] * (_rms(y) * g_ref[3]) for r, y in zip(rows, ys)]
    hs = [_ffn_in(x2, g_ref[4], mod_ref[6], mod_ref[7]) for x2 in x2s]
    accs = [_swiglu(h, wup_ref, wdn_ref) for h in hs]
    for r, x2, acc in zip(rows, x2s, accs):
        out_ref[r, :] = _ffn_out(x2, acc, g_ref[5], mod_ref[8])


def _out_ffn(x1, conv, hsum, o, mod4, g3, mhn, wo, wup, wdn, seq, tm):
    n, d = x1.shape
    tpb = seq // tm
    tile = lambda w: pl.BlockSpec((tm, w), lambda i: (i, 0))
    return pl.pallas_call(
        _out_ffn_kernel,
        grid=(n // tm,),
        in_specs=[tile(d), tile(CONV_W), tile(MLSTM_W), tile(MLSTM_W),
                  pl.BlockSpec((None, N_MOD, 1, d), lambda i: (i // tpb, 0, 0, 0)),
                  _resident(g3.shape), _resident(mhn.shape), _resident(wo.shape), _resident(wup.shape),
                  _resident(wdn.shape)],
        out_specs=tile(d),
        out_shape=jax.ShapeDtypeStruct((n, d), F32),
        compiler_params=_params(),
        name="outproj_ffn2",
    )(x1, conv, hsum, o, mod4, g3, mhn, wo, wup, wdn)


def kernel(x, c, ctx, c_ctx, w_mod, b_mod, norm_g, ffn1_up, ffn1_down, ffn2_up, ffn2_down,
           w_in, b_gates, conv_w, mh_norm, w_out):
    nb, seq, d = x.shape
    ctx_len = ctx.shape[1]
    depth = w_mod.shape[0]
    assert depth == 1, "only the single (last) layer configuration is implemented"
    assert nb + 1 <= MOD_ROWS and seq % GRID_W == 0
    tm_in, tm_out = min(512, seq), min(1024, seq)
    assert tm_in % SUB_ROWS == 0 and tm_out % SUB_ROWS == 0 and SUB_ROWS % GRID_W == 0
    conv_cols = 3 * CONV_W

    cvec = jnp.concatenate([c, c_ctx[None], jnp.zeros((MOD_ROWS - nb - 1, d), F32)], axis=0)
    mod4 = _modulation(cvec, w_mod[0], b_mod[0]).reshape(MOD_ROWS, N_MOD, 1, d)

    g3 = norm_g[0].reshape(6, 1, d)
    w1u, w1d = ffn1_up[0].astype(BF16), ffn1_down[0].astype(BF16)
    win = w_in[0]
    seg = lambda j: win[:, conv_cols + j * MLSTM_W:conv_cols + (j + 1) * MLSTM_W]
    wc = win[:, :conv_cols].astype(BF16)
    wqvo = jnp.concatenate([seg(0), seg(2), seg(3)], axis=1).astype(BF16)
    wv = seg(2).astype(BF16)
    wkt = seg(1).T.astype(BF16)
    wg = win[:, conv_cols + 4 * MLSTM_W:].T.astype(BF16)
    bg = b_gates[0].reshape(N_GATES, 1)
    cw3 = conv_w[0].reshape(3, 1, CONV_W)
    mh = mh_norm[0].reshape(1, MLSTM_W)

    (x1, conv, q, v, o, kt, gt), (wo, w2u, w2d) = _ffn_proj(
        x.reshape(nb * seq, d), mod4, g3, w1u, w1d, wc, wqvo, wkt, wg, bg, cw3,
        (w_out, ffn2_up, ffn2_down), seq, tm_in)
    cv, ckt, cgt = _ffn_proj_ctx(ctx.reshape(nb * ctx_len, d), mod4, g3, w1u, w1d, wv, wkt, wg, bg,
                                 ctx_len, nb)
    hsum = _mlstm(q.reshape(nb, seq, MLSTM_W), v.reshape(nb, seq, MLSTM_W), kt, gt,
                  cv.reshape(nb, ctx_len, MLSTM_W), ckt, cgt)
    out = _out_ffn(x1, conv, hsum.reshape(nb * seq, MLSTM_W), o, mod4, g3, mh, wo, w2u, w2d, seq, tm_out)
    return out.reshape(nb, seq, d)
```

```python
import jax
import jax.numpy as jnp
from jax import lax
from jax.experimental import pallas as pl
from jax.experimental.pallas import tpu as pltpu

F32 = jnp.float32
BF16 = jnp.bfloat16

GRID_W = 64
CONV_W = 512
N_HEADS = 4
HEAD_DIM = 128
MLSTM_W = N_HEADS * HEAD_DIM
CHUNK = 128
N_MOD = 9
EPS = 1e-6
LOG2E = 1.4426950408889634
N_GATES = 4 * N_HEADS
MOD_ROWS = 24
VMEM_LIMIT = 56 * 1024 * 1024

FFN_CHUNK = 512
BF16_SUBLANES = 16
SUB_ROWS = 256
TILE_ROWS_IN = 512
TILE_ROWS_OUT = 1024
CTX_PER_STEP = 2

ROW_A, ROW_M, ROW_E, ROW_W, ROW_DECAY, ROW_MIN, N_ROWS = 0, 1, 2, 3, 4, 5, 8


def _dot(a, b):
    return jnp.dot(a, b, preferred_element_type=F32)


def _dot_nt(a, b):
    return lax.dot_general(a, b, (((1,), (1,)), ((), ())), preferred_element_type=F32)


def _rms(x):
    return x * lax.rsqrt(jnp.mean(x * x, axis=-1, keepdims=True) + EPS)


def _params():
    return pltpu.CompilerParams(dimension_semantics=("arbitrary",), vmem_limit_bytes=VMEM_LIMIT)


def _resident(shape):
    nd = len(shape)
    return pl.BlockSpec(shape, lambda *_: (0,) * nd, pipeline_mode=pl.Buffered(1))


def _mod_kernel(c_ref, w_ref, b_ref, o_ref):
    cv = c_ref[...]
    s = (cv * jax.nn.sigmoid(cv)).astype(BF16)
    o_ref[...] = _dot(s, w_ref[...].astype(BF16)) + b_ref[...]


def _modulation(cvec, w_mod, b_mod):
    d = cvec.shape[1]
    n = w_mod.shape[1]
    bn = n // N_MOD
    return pl.pallas_call(
        _mod_kernel,
        grid=(N_MOD,),
        in_specs=[pl.BlockSpec((MOD_ROWS, d), lambda j: (0, 0)),
                  pl.BlockSpec((d, bn), lambda j: (0, j)),
                  pl.BlockSpec((1, bn), lambda j: (0, j))],
        out_specs=pl.BlockSpec((MOD_ROWS, bn), lambda j: (0, j)),
        out_shape=jax.ShapeDtypeStruct((MOD_ROWS, n), F32),
        compiler_params=_params(),
        name="modulation",
    )(cvec, w_mod, b_mod.reshape(1, n))


def _ffn_in(x, g_pre, shift, scale):
    return (_rms(x) * g_pre * (1.0 + scale) + shift).astype(BF16)


def _swiglu(h, wup_ref, wdn_ref):
    ffn = wdn_ref.shape[0]
    chunks = [(c0, min(FFN_CHUNK, ffn - c0)) for c0 in range(0, ffn, FFN_CHUNK)]

    def up(c0, cw):
        return _dot(h, wup_ref[:, c0:c0 + cw]), _dot(h, wup_ref[:, ffn + c0:ffn + c0 + cw])

    acc = None
    ab = up(*chunks[0])
    for i, (c0, cw) in enumerate(chunks):
        a, b = ab
        if i + 1 < len(chunks):
            ab = up(*chunks[i + 1])
        act = (a * jax.nn.sigmoid(a) * b).astype(BF16)
        part = _dot(act, wdn_ref[c0:c0 + cw, :])
        acc = part if acc is None else acc + part
    return acc


def _ffn_out(x, acc, g_post, gate):
    return x + 0.5 * gate * (_rms(acc) * g_post)


def _sub_rows(n_rows):
    return [pl.ds(r, SUB_ROWS) for r in range(0, n_rows, SUB_ROWS)]


def _log_sigmoid(x):
    return jnp.minimum(x, 0.0) - jnp.log1p(jnp.exp(-jnp.abs(x)))


def _gates_t(h, wg_ref, bg_ref):
    g = _dot_nt(wg_ref[...], h) + bg_ref[...]
    half = N_GATES // 2
    return jnp.concatenate([g[:half], _log_sigmoid(g[half:])], axis=0)


def _mixer_in(x1, g_ref, mod_ref):
    return (_rms(x1) * g_ref[2] * (1.0 + mod_ref[4]) + mod_ref[3]).astype(BF16)


def _ffn_proj_kernel(x_ref, mod_ref, g_ref, wup_ref, wdn_ref, wc_ref, wqvo_ref, wkt_ref, wg_ref, bg_ref,
                     cw_ref, *refs):
    n_cast = (len(refs) - 5) // 2
    cast_in, (x1_ref, conv_ref, qvo_ref, kt_ref, gt_ref), cast_out = refs[:n_cast], refs[n_cast:n_cast + 5], \
        refs[n_cast + 5:]
    for src, dst in zip(cast_in, cast_out):
        dst[...] = src[...].astype(BF16)

    rows = _sub_rows(x_ref.shape[0])
    hs = [_ffn_in(x_ref[r, :], g_ref[0], mod_ref[0], mod_ref[1]) for r in rows]
    accs = [_swiglu(h, wup_ref, wdn_ref) for h in hs]
    h2s = []
    for r, acc in zip(rows, accs):
        x1 = _ffn_out(x_ref[r, :], acc, g_ref[1], mod_ref[2])
        x1_ref[r, :] = x1
        h2s.append(_mixer_in(x1, g_ref, mod_ref))

    for r, h in zip(rows, h2s):
        sub = h.shape[0]
        bg = _dot(h, wc_ref[:, 0:CONV_W])
        cu = _dot(h, wc_ref[:, CONV_W:2 * CONV_W]) * _dot(h, wc_ref[:, 2 * CONV_W:3 * CONV_W])
        col = lax.broadcasted_iota(jnp.int32, (sub, 1), 0) % GRID_W
        prev = jnp.where(col == 0, 0.0, pltpu.roll(cu, 1, 0))
        nxt = jnp.where(col == GRID_W - 1, 0.0, pltpu.roll(cu, sub - 1, 0))
        conv = bg * (cw_ref[0] * prev + cw_ref[1] * cu + cw_ref[2] * nxt)
        conv_ref[r, :] = conv.astype(BF16)

        for j in range(3):
            cols = slice(j * MLSTM_W, (j + 1) * MLSTM_W)
            qvo_ref[r, cols] = _dot(h, wqvo_ref[:, cols]).astype(BF16)
        kt_ref[:, r] = (_dot_nt(wkt_ref[...], h) * (HEAD_DIM ** -0.5)).astype(BF16)
        gt_ref[:, r] = _gates_t(h, wg_ref, bg_ref)


def _ffn_proj(x2d, mod4, g3, wup, wdn, wc, wqvo, wkt, wg, bg, cw3, to_cast, seq, tm):
    n, d = x2d.shape
    tpb = seq // tm
    nb = n // seq
    steps = n // tm
    tile = lambda w: pl.BlockSpec((tm, w), lambda i: (i, 0))
    tile_t = lambda r: pl.BlockSpec((None, r, tm), lambda i: (i // tpb, 0, i % tpb))
    def slab_specs(w):
        rows = next(r for r in range(BF16_SUBLANES, w.shape[1] + 1, BF16_SUBLANES)
                    if w.shape[1] % r == 0 and w.shape[1] // r <= steps)
        last = w.shape[1] // rows - 1
        return (pl.BlockSpec((None, rows, w.shape[2]), lambda i: (0, jnp.minimum(i, last), 0)),
                pl.BlockSpec((rows, w.shape[2]), lambda i: (jnp.minimum(i, last), 0)))
    cast_specs = [slab_specs(w) for w in to_cast]
    outs = pl.pallas_call(
        _ffn_proj_kernel,
        grid=(steps,),
        in_specs=[tile(d),
                  pl.BlockSpec((None, N_MOD, 1, d), lambda i: (i // tpb, 0, 0, 0)),
                  _resident(g3.shape), _resident(wup.shape), _resident(wdn.shape),
                  _resident(wc.shape), _resident(wqvo.shape), _resident(wkt.shape), _resident(wg.shape),
                  _resident(bg.shape), _resident(cw3.shape)] + [spec[0] for spec in cast_specs],
        out_specs=[tile(d), tile(CONV_W), tile(3 * MLSTM_W), tile_t(MLSTM_W), tile_t(N_GATES)]
        + [spec[1] for spec in cast_specs],
        out_shape=[jax.ShapeDtypeStruct((n, d), F32),
                   jax.ShapeDtypeStruct((n, CONV_W), BF16),
                   jax.ShapeDtypeStruct((n, 3 * MLSTM_W), BF16),
                   jax.ShapeDtypeStruct((nb, MLSTM_W, seq), BF16),
                   jax.ShapeDtypeStruct((nb, N_GATES, seq), F32)]
        + [jax.ShapeDtypeStruct(w.shape[1:], BF16) for w in to_cast],
        compiler_params=_params(),
        name="ffn1_inproj",
    )(x2d, mod4, g3, wup, wdn, wc, wqvo, wkt, wg, bg, cw3, *to_cast)
    return outs[:5], outs[5:]


def _ffn_proj_ctx_kernel(x_ref, mod_ref, g_ref, wup_ref, wdn_ref, wv_ref, wkt_ref, wg_ref, bg_ref,
                         v_ref, kt_ref, gt_ref):
    n_sub, ctx_len = kt_ref.shape[0], kt_ref.shape[2]
    rows = [pl.ds(r * ctx_len, ctx_len) for r in range(n_sub)]
    hs = [_ffn_in(x_ref[r, :], g_ref[0], mod_ref[0], mod_ref[1]) for r in rows]
    accs = [_swiglu(h, wup_ref, wdn_ref) for h in hs]
    h2s = [_mixer_in(_ffn_out(x_ref[r, :], acc, g_ref[1], mod_ref[2]), g_ref, mod_ref)
           for r, acc in zip(rows, accs)]
    for j, (r, h) in enumerate(zip(rows, h2s)):
        v_ref[r, :] = _dot(h, wv_ref[...]).astype(BF16)
        kt_ref[j] = (_dot_nt(wkt_ref[...], h) * (HEAD_DIM ** -0.5)).astype(BF16)
        gt_ref[j] = _gates_t(h, wg_ref, bg_ref)


def _ffn_proj_ctx(c2d, mod4, g3, wup, wdn, wv, wkt, wg, bg, ctx_len, ctx_row):
    n, d = c2d.shape
    assert (n // ctx_len) % CTX_PER_STEP == 0
    tm = CTX_PER_STEP * ctx_len
    tile_t = lambda r: pl.BlockSpec((CTX_PER_STEP, r, ctx_len), lambda i: (i, 0, 0))
    return pl.pallas_call(
        _ffn_proj_ctx_kernel,
        grid=(n // tm,),
        in_specs=[pl.BlockSpec((tm, d), lambda i: (i, 0)),
                  pl.BlockSpec((None, N_MOD, 1, d), lambda i: (ctx_row, 0, 0, 0)),
                  _resident(g3.shape), _resident(wup.shape), _resident(wdn.shape),
                  _resident(wv.shape), _resident(wkt.shape), _resident(wg.shape), _resident(bg.shape)],
        out_specs=[pl.BlockSpec((tm, MLSTM_W), lambda i: (i, 0)), tile_t(MLSTM_W), tile_t(N_GATES)],
        out_shape=[jax.ShapeDtypeStruct((n, MLSTM_W), BF16),
                   jax.ShapeDtypeStruct((n // ctx_len, MLSTM_W, ctx_len), BF16),
                   jax.ShapeDtypeStruct((n // ctx_len, N_GATES, ctx_len), F32)],
        compiler_params=_params(),
        name="ffn1_inproj_ctx",
    )(c2d, mod4, g3, wup, wdn, wv, wkt, wg, bg)


def _lane_scan(x, op, reverse):
    pos = lax.broadcasted_iota(jnp.int32, x.shape, 1)
    d = 1
    while d < CHUNK:
        if reverse:
            x = jnp.where(pos < CHUNK - d, op(x, pltpu.roll(x, CHUNK - d, 1)), x)
        else:
            x = jnp.where(pos >= d, op(x, pltpu.roll(x, d, 1)), x)
        d *= 2
    return x


def _lane_pick(x, lane):
    idx = lax.broadcasted_iota(jnp.int32, x.shape, 1)
    return jnp.sum(jnp.where(idx == lane, x, 0.0), axis=1, keepdims=True)


def _chunk_rows(c):
    if isinstance(c, int):
        return pl.ds(c * CHUNK, CHUNK)
    return pl.ds(pl.multiple_of(c * CHUNK, CHUNK), CHUNK)


def _gate_scan(ig, lf, m, reverse):
    n = ig.shape[0]
    b = _lane_scan(lf, jnp.add, reverse)
    a = ig - b
    b_last = _lane_pick(b, 0 if reverse else CHUNK - 1)
    wmax = b_last + jnp.max(a, axis=1, keepdims=True)
    sub = lax.broadcasted_iota(jnp.int32, (n, 1), 0)
    m_in = jnp.zeros((n, 1), F32)
    m_out = jnp.zeros((n, 1), F32)
    for c in (reversed(range(n)) if reverse else range(n)):
        m_new = jnp.maximum(b_last[c:c + 1] + m, wmax[c:c + 1])
        m_in = jnp.where(sub == c, m, m_in)
        m_out = jnp.where(sub == c, m_new, m_out)
        m = m_new
    big_m = jnp.maximum(m_in, _lane_scan(a, jnp.maximum, reverse))
    e = jnp.exp(-(b + big_m))
    w = jnp.exp(a + b_last - m_out)
    decay = jnp.exp(b_last + m_in - m_out)
    return (a, big_m, e, w, decay, m_in), m


def _v_aug(v):
    return jnp.concatenate([v, jnp.ones((CHUNK, HEAD_DIM), BF16)], axis=1)


def _mlstm_kernel(q_ref, kt_ref, v_ref, g_ref, ckt_ref, cv_ref, cg_ref, out_ref, s_ref, rows_ref, neg_ref):
    nc = kt_ref.shape[1] // CHUNK
    ncc = ckt_ref.shape[1] // CHUNK

    for h in range(N_HEADS):
        hs = slice(h * HEAD_DIM, (h + 1) * HEAD_DIM)
        for d in range(2):
            hd = 2 * h + d
            rev = d == 1
            i_row, f_row = d * N_HEADS + h, (2 + d) * N_HEADS + h
            (_, _, _, w, decay, _), m = _gate_scan(cg_ref[i_row], cg_ref[f_row], jnp.zeros((1, 1), F32), rev)
            s = jnp.zeros((HEAD_DIM, 2 * HEAD_DIM), F32)
            for c in (reversed(range(ncc)) if rev else range(ncc)):
                tok = _chunk_rows(c)
                kw = (ckt_ref[hs, tok].astype(F32) * w[c:c + 1]).astype(BF16)
                s = decay[c:c + 1] * s + _dot(kw, _v_aug(cv_ref[tok, hs]))
            s_ref[hd] = s
            (a, big_m, e, w, decay, m_in), _ = _gate_scan(g_ref[i_row], g_ref[f_row], m, rev)
            decay = jnp.broadcast_to(decay, (nc, CHUNK))
            m_in = jnp.broadcast_to(m_in, (nc, CHUNK))
            vals = ((ROW_A, a * LOG2E), (ROW_M, big_m * LOG2E), (ROW_E, e), (ROW_W, w),
                    (ROW_DECAY, decay), (ROW_MIN, m_in * LOG2E))
            for c in range(nc):
                for r, val in vals:
                    rows_ref[hd, c, r:r + 1, :] = val[c:c + 1]

    row_id = lax.broadcasted_iota(jnp.int32, (CHUNK, CHUNK), 0)
    col_id = lax.broadcasted_iota(jnp.int32, (CHUNK, CHUNK), 1)
    neg_ref[0] = jnp.where(col_id <= row_id, 0.0, -jnp.inf)
    neg_ref[1] = jnp.where(col_id >= row_id, 0.0, -jnp.inf)

    def to_col(row):
        return jnp.sum(jnp.where(row_id == col_id, row, 0.0), axis=1, keepdims=True)

    def scores(h, d, c):
        hs = slice(h * HEAD_DIM, (h + 1) * HEAD_DIM)
        tok = _chunk_rows(c)
        rows = rows_ref[2 * h + d, c]
        m_col = to_col(rows[ROW_M:ROW_M + 1])
        g_col = jnp.exp2(rows[ROW_MIN:ROW_MIN + 1] - m_col)
        dmat = jnp.exp2(rows[ROW_A:ROW_A + 1] - m_col + neg_ref[d])
        q = q_ref[tok, hs]
        kt = kt_ref[hs, tok]
        lhs = jnp.concatenate([(_dot(q, kt) * dmat).astype(BF16),
                               (q.astype(F32) * g_col).astype(BF16)], axis=1)
        kw = (kt.astype(F32) * rows[ROW_W:ROW_W + 1]).astype(BF16)
        return lhs, kw, rows

    def readout(h, d, c, lhs, kw, rows):
        hs = slice(h * HEAD_DIM, (h + 1) * HEAD_DIM)
        v_aug = _v_aug(v_ref[_chunk_rows(c), hs])
        s_old = s_ref[2 * h + d]
        tot = _dot(lhs, jnp.concatenate([v_aug, s_old.astype(BF16)], axis=0))
        dec = rows[ROW_DECAY:ROW_DECAY + 1]
        s_ref[2 * h + d] = jnp.concatenate([dec, dec], axis=1) * s_old + _dot(kw, v_aug)
        return tot

    def normalise(tot, rows):
        e_col = to_col(rows[ROW_E:ROW_E + 1])
        return tot[:, :HEAD_DIM] / jnp.maximum(jnp.abs(tot[:, HEAD_DIM:]), e_col)

    def iteration(i, accumulate):
        scans = [(h, d, (nc - 1 - i) if d else i) for h in range(N_HEADS) for d in range(2)]
        staged = [scores(*s) for s in scans]
        tots = [readout(*s, *st) for s, st in zip(scans, staged)]
        for (h, d, c), tot, st in zip(scans, tots, staged):
            hh = normalise(tot, st[2])
            dst = (_chunk_rows(c), slice(h * HEAD_DIM, (h + 1) * HEAD_DIM))
            out_ref[dst] = out_ref[dst] + hh if accumulate else hh

    def first_half(i, carry):
        iteration(i, False)
        return carry

    def second_half(i, carry):
        iteration(i, True)
        return carry

    lax.fori_loop(0, nc // 2, first_half, 0, unroll=2)
    lax.fori_loop(nc // 2, nc, second_half, 0, unroll=2)


def _mlstm(qvo, kt, gt, cv, ckt, cgt):
    nb, seq, _ = qvo.shape
    ctx_len = cv.shape[1]
    assert seq % (2 * CHUNK) == 0 and ctx_len % CHUNK == 0
    nc, ncc = seq // CHUNK, ctx_len // CHUNK
    qvo_blk = lambda j: pl.BlockSpec((None, seq, MLSTM_W), lambda b: (b, 0, j))
    whole = lambda *s: pl.BlockSpec((None,) + s, lambda b: (b,) + (0,) * len(s))
    return pl.pallas_call(
        _mlstm_kernel,
        grid=(nb,),
        in_specs=[qvo_blk(0), whole(MLSTM_W, seq), qvo_blk(1), whole(N_GATES, nc, CHUNK),
                  whole(MLSTM_W, ctx_len), whole(ctx_len, MLSTM_W), whole(N_GATES, ncc, CHUNK)],
        out_specs=whole(seq, MLSTM_W),
        out_shape=jax.ShapeDtypeStruct((nb, seq, MLSTM_W), F32),
        scratch_shapes=[pltpu.VMEM((2 * N_HEADS, HEAD_DIM, 2 * HEAD_DIM), F32),
                        pltpu.VMEM((2 * N_HEADS, nc, N_ROWS, CHUNK), F32),
                        pltpu.VMEM((2, CHUNK, CHUNK), F32)],
        compiler_params=_params(),
        name="mlstm",
    )(qvo, kt, qvo, gt.reshape(nb, N_GATES, nc, CHUNK), ckt, cv, cgt.reshape(nb, N_GATES, ncc, CHUNK))


def _mlstm_out(h, o, mh_norm):
    hn = jnp.concatenate([_rms(h[:, j * HEAD_DIM:(j + 1) * HEAD_DIM]) for j in range(N_HEADS)], axis=1)
    return (jax.nn.sigmoid(o.astype(F32)) * (hn * mh_norm)).astype(BF16)


def _out_ffn_kernel(x1_ref, conv_ref, h_ref, o_ref, mod_ref, g_ref, mhn_ref, wo_ref, wup_ref, wdn_ref, out_ref):
    rows = _sub_rows(x1_ref.shape[0])
    hms = [_mlstm_out(h_ref[r, :], o_ref[r, :], mhn_ref[...]) for r in rows]
    ys = [_dot(conv_ref[r, :], wo_ref[0:CONV_W, :]) + _dot(hm, wo_ref[CONV_W:, :]) for r, hm in zip(rows, hms)]
    x2s = [x1_ref[r, :] + mod_ref[5] * (_rms(y) * g_ref[3]) for r, y in zip(rows, ys)]
    hs = [_ffn_in(x2, g_ref[4], mod_ref[6], mod_ref[7]) for x2 in x2s]
    accs = [_swiglu(h, wup_ref, wdn_ref) for h in hs]
    for r, x2, acc in zip(rows, x2s, accs):
        out_ref[r, :] = _ffn_out(x2, acc, g_ref[5], mod_ref[8])


def _out_ffn(x1, conv, hsum, qvo, mod4, g3, mhn, wo, wup, wdn, seq, tm):
    n, d = x1.shape
    tpb = seq // tm
    tile = lambda w: pl.BlockSpec((tm, w), lambda i: (i, 0))
    return pl.pallas_call(
        _out_ffn_kernel,
        grid=(n // tm,),
        in_specs=[tile(d), tile(CONV_W), tile(MLSTM_W),
                  pl.BlockSpec((tm, MLSTM_W), lambda i: (i, 2)),
                  pl.BlockSpec((None, N_MOD, 1, d), lambda i: (i // tpb, 0, 0, 0)),
                  _resident(g3.shape), _resident(mhn.shape), _resident(wo.shape), _resident(wup.shape),
                  _resident(wdn.shape)],
        out_specs=tile(d),
        out_shape=jax.ShapeDtypeStruct((n, d), F32),
        compiler_params=_params(),
        name="outproj_ffn2",
    )(x1, conv, hsum, qvo, mod4, g3, mhn, wo, wup, wdn)


def kernel(x, c, ctx, c_ctx, w_mod, b_mod, norm_g, ffn1_up, ffn1_down, ffn2_up, ffn2_down,
           w_in, b_gates, conv_w, mh_norm, w_out):
    nb, seq, d = x.shape
    ctx_len = ctx.shape[1]
    depth = w_mod.shape[0]
    assert depth == 1, "only the single (last) layer configuration is implemented"
    assert nb + 1 <= MOD_ROWS and seq % GRID_W == 0
    tm_in, tm_out = min(TILE_ROWS_IN, seq), min(TILE_ROWS_OUT, seq)
    assert tm_in % SUB_ROWS == 0 and tm_out % SUB_ROWS == 0 and SUB_ROWS % GRID_W == 0
    conv_cols = 3 * CONV_W

    cvec = jnp.concatenate([c, c_ctx[None], jnp.zeros((MOD_ROWS - nb - 1, d), F32)], axis=0)
    mod4 = _modulation(cvec, w_mod[0], b_mod[0]).reshape(MOD_ROWS, N_MOD, 1, d)

    g3 = norm_g[0].reshape(6, 1, d)
    w1u, w1d = ffn1_up[0].astype(BF16), ffn1_down[0].astype(BF16)
    win = w_in[0]
    seg = lambda j: win[:, conv_cols + j * MLSTM_W:conv_cols + (j + 1) * MLSTM_W]
    wc = win[:, :conv_cols].astype(BF16)
    wqvo = jnp.concatenate([seg(0), seg(2), seg(3)], axis=1).astype(BF16)
    wv = seg(2).astype(BF16)
    wkt = seg(1).T.astype(BF16)
    wg = win[:, conv_cols + 4 * MLSTM_W:].T.astype(BF16)
    bg = b_gates[0].reshape(N_GATES, 1)
    cw3 = conv_w[0].reshape(3, 1, CONV_W)
    mh = mh_norm[0].reshape(1, MLSTM_W)

    (x1, conv, qvo, kt, gt), (wo, w2u, w2d) = _ffn_proj(
        x.reshape(nb * seq, d), mod4, g3, w1u, w1d, wc, wqvo, wkt, wg, bg, cw3,
        (w_out, ffn2_up, ffn2_down), seq, tm_in)
    cv, ckt, cgt = _ffn_proj_ctx(ctx.reshape(nb * ctx_len, d), mod4, g3, w1u, w1d, wv, wkt, wg, bg,
                                 ctx_len, nb)
    hsum = _mlstm(qvo.reshape(nb, seq, 3 * MLSTM_W), kt, gt, cv.reshape(nb, ctx_len, MLSTM_W), ckt, cgt)
    out = _out_ffn(x1, conv, hsum.reshape(nb * seq, MLSTM_W), qvo, mod4, g3, mh, wo, w2u, w2d, seq, tm_out)
    return out.reshape(nb, seq, d)
```

```python
import jax
import jax.numpy as jnp
from jax import lax
from jax.experimental import pallas as pl
from jax.experimental.pallas import tpu as pltpu

F32 = jnp.float32
BF16 = jnp.bfloat16

GRID_W = 64
CONV_W = 512
N_HEADS = 4
HEAD_DIM = 128
MLSTM_W = N_HEADS * HEAD_DIM
CHUNK = 128
N_MOD = 9
EPS = 1e-6
LOG2E = 1.4426950408889634
N_GATES = 4 * N_HEADS
MOD_ROWS = 24
VMEM_LIMIT = 56 * 1024 * 1024

FFN_CHUNK = 512
BF16_SUBLANES = 16
SUB_ROWS = 256
TILE_ROWS_IN = 512
TILE_ROWS_OUT = 1024
CTX_PER_STEP = 2

ROW_A, ROW_M, ROW_E, ROW_W, ROW_DECAY, ROW_MIN, N_ROWS = 0, 1, 2, 3, 4, 5, 8


def _dot(a, b):
    return jnp.dot(a, b, preferred_element_type=F32)


def _dot_nt(a, b):
    return lax.dot_general(a, b, (((1,), (1,)), ((), ())), preferred_element_type=F32)


def _rms(x):
    return x * lax.rsqrt(jnp.mean(x * x, axis=-1, keepdims=True) + EPS)


def _params():
    return pltpu.CompilerParams(dimension_semantics=("arbitrary",), vmem_limit_bytes=VMEM_LIMIT)


def _resident(shape):
    nd = len(shape)
    return pl.BlockSpec(shape, lambda *_: (0,) * nd, pipeline_mode=pl.Buffered(1))


def _mod_kernel(c_ref, w_ref, b_ref, o_ref):
    cv = c_ref[...]
    s = (cv * jax.nn.sigmoid(cv)).astype(BF16)
    o_ref[...] = _dot(s, w_ref[...].astype(BF16)) + b_ref[...]


def _modulation(cvec, w_mod, b_mod):
    d = cvec.shape[1]
    n = w_mod.shape[1]
    bn = n // N_MOD
    return pl.pallas_call(
        _mod_kernel,
        grid=(N_MOD,),
        in_specs=[pl.BlockSpec((MOD_ROWS, d), lambda j: (0, 0)),
                  pl.BlockSpec((d, bn), lambda j: (0, j)),
                  pl.BlockSpec((1, bn), lambda j: (0, j))],
        out_specs=pl.BlockSpec((MOD_ROWS, bn), lambda j: (0, j)),
        out_shape=jax.ShapeDtypeStruct((MOD_ROWS, n), F32),
        compiler_params=_params(),
        name="modulation",
    )(cvec, w_mod, b_mod.reshape(1, n))


def _ffn_in(x, g_pre, shift, scale):
    return (_rms(x) * g_pre * (1.0 + scale) + shift).astype(BF16)


def _swiglu(h, wup_ref, wdn_ref):
    ffn = wdn_ref.shape[0]
    chunks = [(c0, min(FFN_CHUNK, ffn - c0)) for c0 in range(0, ffn, FFN_CHUNK)]

    def up(c0, cw):
        return _dot(h, wup_ref[:, c0:c0 + cw]), _dot(h, wup_ref[:, ffn + c0:ffn + c0 + cw])

    acc = None
    ab = up(*chunks[0])
    for i, (c0, cw) in enumerate(chunks):
        a, b = ab
        if i + 1 < len(chunks):
            ab = up(*chunks[i + 1])
        act = (a * jax.nn.sigmoid(a) * b).astype(BF16)
        part = _dot(act, wdn_ref[c0:c0 + cw, :])
        acc = part if acc is None else acc + part
    return acc


def _ffn_out(x, acc, g_post, gate):
    return x + 0.5 * gate * (_rms(acc) * g_post)


def _sub_rows(n_rows):
    return [pl.ds(r, SUB_ROWS) for r in range(0, n_rows, SUB_ROWS)]


def _log_sigmoid(x):
    return jnp.minimum(x, 0.0) - jnp.log1p(jnp.exp(-jnp.abs(x)))


def _gates_t(h, wg_ref, bg_ref):
    g = _dot_nt(wg_ref[...], h) + bg_ref[...]
    half = N_GATES // 2
    return jnp.concatenate([g[:half], _log_sigmoid(g[half:])], axis=0)


def _mixer_in(x1, g_ref, mod_ref):
    return (_rms(x1) * g_ref[2] * (1.0 + mod_ref[4]) + mod_ref[3]).astype(BF16)


def _ffn_proj_kernel(x_ref, mod_ref, g_ref, wup_ref, wdn_ref, wc_ref, wqvo_ref, wkt_ref, wg_ref, bg_ref,
                     cw_ref, *refs):
    n_cast = (len(refs) - 5) // 2
    cast_in, (x1_ref, conv_ref, qvo_ref, kt_ref, gt_ref), cast_out = refs[:n_cast], refs[n_cast:n_cast + 5], \
        refs[n_cast + 5:]
    for src, dst in zip(cast_in, cast_out):
        dst[...] = src[...].astype(BF16)

    rows = _sub_rows(x_ref.shape[0])
    hs = [_ffn_in(x_ref[r, :], g_ref[0], mod_ref[0], mod_ref[1]) for r in rows]
    accs = [_swiglu(h, wup_ref, wdn_ref) for h in hs]
    h2s = []
    for r, acc in zip(rows, accs):
        x1 = _ffn_out(x_ref[r, :], acc, g_ref[1], mod_ref[2])
        x1_ref[r, :] = x1
        h2s.append(_mixer_in(x1, g_ref, mod_ref))

    for r, h in zip(rows, h2s):
        sub = h.shape[0]
        bg = _dot(h, wc_ref[:, 0:CONV_W])
        cu = _dot(h, wc_ref[:, CONV_W:2 * CONV_W]) * _dot(h, wc_ref[:, 2 * CONV_W:3 * CONV_W])
        col = lax.broadcasted_iota(jnp.int32, (sub, 1), 0) % GRID_W
        prev = jnp.where(col == 0, 0.0, pltpu.roll(cu, 1, 0))
        nxt = jnp.where(col == GRID_W - 1, 0.0, pltpu.roll(cu, sub - 1, 0))
        conv = bg * (cw_ref[0] * prev + cw_ref[1] * cu + cw_ref[2] * nxt)
        conv_ref[r, :] = conv.astype(BF16)

        for j in range(3):
            cols = slice(j * MLSTM_W, (j + 1) * MLSTM_W)
            qvo_ref[r, cols] = _dot(h, wqvo_ref[:, cols]).astype(BF16)
        kt_ref[:, r] = (_dot_nt(wkt_ref[...], h) * (HEAD_DIM ** -0.5)).astype(BF16)
        gt_ref[:, r] = _gates_t(h, wg_ref, bg_ref)


def _ffn_proj(x2d, mod4, g3, wup, wdn, wc, wqvo, wkt, wg, bg, cw3, to_cast, seq, tm):
    n, d = x2d.shape
    tpb = seq // tm
    nb = n // seq
    steps = n // tm
    tile = lambda w: pl.BlockSpec((tm, w), lambda i: (i, 0))
    tile_t = lambda r: pl.BlockSpec((None, r, tm), lambda i: (i // tpb, 0, i % tpb))
    def slab_specs(w):
        rows = next(r for r in range(BF16_SUBLANES, w.shape[1] + 1, BF16_SUBLANES)
                    if w.shape[1] % r == 0 and w.shape[1] // r <= steps)
        last = w.shape[1] // rows - 1
        return (pl.BlockSpec((None, rows, w.shape[2]), lambda i: (0, jnp.minimum(i, last), 0)),
                pl.BlockSpec((rows, w.shape[2]), lambda i: (jnp.minimum(i, last), 0)))
    cast_specs = [slab_specs(w) for w in to_cast]
    outs = pl.pallas_call(
        _ffn_proj_kernel,
        grid=(steps,),
        in_specs=[tile(d),
                  pl.BlockSpec((None, N_MOD, 1, d), lambda i: (i // tpb, 0, 0, 0)),
                  _resident(g3.shape), _resident(wup.shape), _resident(wdn.shape),
                  _resident(wc.shape), _resident(wqvo.shape), _resident(wkt.shape), _resident(wg.shape),
                  _resident(bg.shape), _resident(cw3.shape)] + [spec[0] for spec in cast_specs],
        out_specs=[tile(d), tile(CONV_W), tile(3 * MLSTM_W), tile_t(MLSTM_W), tile_t(N_GATES)]
        + [spec[1] for spec in cast_specs],
        out_shape=[jax.ShapeDtypeStruct((n, d), F32),
                   jax.ShapeDtypeStruct((n, CONV_W), BF16),
                   jax.ShapeDtypeStruct((n, 3 * MLSTM_W), BF16),
                   jax.ShapeDtypeStruct((nb, MLSTM_W, seq), BF16),
                   jax.ShapeDtypeStruct((nb, N_GATES, seq), F32)]
        + [jax.ShapeDtypeStruct(w.shape[1:], BF16) for w in to_cast],
        compiler_params=_params(),
        name="ffn1_inproj",
    )(x2d, mod4, g3, wup, wdn, wc, wqvo, wkt, wg, bg, cw3, *to_cast)
    return outs[:5], outs[5:]


def _ffn_proj_ctx_kernel(x_ref, mod_ref, g_ref, wup_ref, wdn_ref, wv_ref, wkt_ref, wg_ref, bg_ref,
                         v_ref, kt_ref, gt_ref):
    n_sub, ctx_len = kt_ref.shape[0], kt_ref.shape[2]
    rows = [pl.ds(r * ctx_len, ctx_len) for r in range(n_sub)]
    hs = [_ffn_in(x_ref[r, :], g_ref[0], mod_ref[0], mod_ref[1]) for r in rows]
    accs = [_swiglu(h, wup_ref, wdn_ref) for h in hs]
    h2s = [_mixer_in(_ffn_out(x_ref[r, :], acc, g_ref[1], mod_ref[2]), g_ref, mod_ref)
           for r, acc in zip(rows, accs)]
    for j, (r, h) in enumerate(zip(rows, h2s)):
        v_ref[r, :] = _dot(h, wv_ref[...]).astype(BF16)
        kt_ref[j] = (_dot_nt(wkt_ref[...], h) * (HEAD_DIM ** -0.5)).astype(BF16)
        gt_ref[j] = _gates_t(h, wg_ref, bg_ref)


def _ffn_proj_ctx(c2d, mod4, g3, wup, wdn, wv, wkt, wg, bg, ctx_len, ctx_row):
    n, d = c2d.shape
    assert (n // ctx_len) % CTX_PER_STEP == 0
    tm = CTX_PER_STEP * ctx_len
    tile_t = lambda r: pl.BlockSpec((CTX_PER_STEP, r, ctx_len), lambda i: (i, 0, 0))
    return pl.pallas_call(
        _ffn_proj_ctx_kernel,
        grid=(n // tm,),
        in_specs=[pl.BlockSpec((tm, d), lambda i: (i, 0)),
                  pl.BlockSpec((None, N_MOD, 1, d), lambda i: (ctx_row, 0, 0, 0)),
                  _resident(g3.shape), _resident(wup.shape), _resident(wdn.shape),
                  _resident(wv.shape), _resident(wkt.shape), _resident(wg.shape), _resident(bg.shape)],
        out_specs=[pl.BlockSpec((tm, MLSTM_W), lambda i: (i, 0)), tile_t(MLSTM_W), tile_t(N_GATES)],
        out_shape=[jax.ShapeDtypeStruct((n, MLSTM_W), BF16),
                   jax.ShapeDtypeStruct((n // ctx_len, MLSTM_W, ctx_len), BF16),
                   jax.ShapeDtypeStruct((n // ctx_len, N_GATES, ctx_len), F32)],
        compiler_params=_params(),
        name="ffn1_inproj_ctx",
    )(c2d, mod4, g3, wup, wdn, wv, wkt, wg, bg)


def _lane_scan(x, op, reverse):
    pos = lax.broadcasted_iota(jnp.int32, x.shape, 1)
    d = 1
    while d < CHUNK:
        if reverse:
            x = jnp.where(pos < CHUNK - d, op(x, pltpu.roll(x, CHUNK - d, 1)), x)
        else:
            x = jnp.where(pos >= d, op(x, pltpu.roll(x, d, 1)), x)
        d *= 2
    return x


def _lane_pick(x, lane):
    idx = lax.broadcasted_iota(jnp.int32, x.shape, 1)
    return jnp.sum(jnp.where(idx == lane, x, 0.0), axis=1, keepdims=True)


def _chunk_rows(c):
    if isinstance(c, int):
        return pl.ds(c * CHUNK, CHUNK)
    return pl.ds(pl.multiple_of(c * CHUNK, CHUNK), CHUNK)


def _gate_scan(ig, lf, m, reverse):
    n = ig.shape[0]
    b = _lane_scan(lf, jnp.add, reverse)
    a = ig - b
    b_last = _lane_pick(b, 0 if reverse else CHUNK - 1)
    wmax = b_last + jnp.max(a, axis=1, keepdims=True)
    sub = lax.broadcasted_iota(jnp.int32, (n, 1), 0)
    m_in = jnp.zeros((n, 1), F32)
    m_out = jnp.zeros((n, 1), F32)
    for c in (reversed(range(n)) if reverse else range(n)):
        m_new = jnp.maximum(b_last[c:c + 1] + m, wmax[c:c + 1])
        m_in = jnp.where(sub == c, m, m_in)
        m_out = jnp.where(sub == c, m_new, m_out)
        m = m_new
    big_m = jnp.maximum(m_in, _lane_scan(a, jnp.maximum, reverse))
    e = jnp.exp(-(b + big_m))
    w = jnp.exp(a + b_last - m_out)
    decay = jnp.exp(b_last + m_in - m_out)
    return (a, big_m, e, w, decay, m_in), m


def _v_aug(v):
    return jnp.concatenate([v, jnp.ones((CHUNK, HEAD_DIM), BF16)], axis=1)


def _mlstm_kernel(q_ref, kt_ref, v_ref, g_ref, ckt_ref, cv_ref, cg_ref, out_ref, hbuf, s_ref, rows_ref, neg_ref):
    nc = kt_ref.shape[1] // CHUNK
    ncc = ckt_ref.shape[1] // CHUNK

    for h in range(N_HEADS):
        hs = slice(h * HEAD_DIM, (h + 1) * HEAD_DIM)
        for d in range(2):
            hd = 2 * h + d
            rev = d == 1
            i_row, f_row = d * N_HEADS + h, (2 + d) * N_HEADS + h
            (_, _, _, w, decay, _), m = _gate_scan(cg_ref[i_row], cg_ref[f_row], jnp.zeros((1, 1), F32), rev)
            s = jnp.zeros((HEAD_DIM, 2 * HEAD_DIM), F32)
            for c in (reversed(range(ncc)) if rev else range(ncc)):
                tok = _chunk_rows(c)
                kw = (ckt_ref[hs, tok].astype(F32) * w[c:c + 1]).astype(BF16)
                s = decay[c:c + 1] * s + _dot(kw, _v_aug(cv_ref[tok, hs]))
            s_ref[hd] = s
            (a, big_m, e, w, decay, m_in), _ = _gate_scan(g_ref[i_row], g_ref[f_row], m, rev)
            decay = jnp.broadcast_to(decay, (nc, CHUNK))
            m_in = jnp.broadcast_to(m_in, (nc, CHUNK))
            vals = ((ROW_A, a * LOG2E), (ROW_M, big_m * LOG2E), (ROW_E, e), (ROW_W, w),
                    (ROW_DECAY, decay), (ROW_MIN, m_in * LOG2E))
            for c in range(nc):
                for r, val in vals:
                    rows_ref[hd, c, r:r + 1, :] = val[c:c + 1]

    row_id = lax.broadcasted_iota(jnp.int32, (CHUNK, CHUNK), 0)
    col_id = lax.broadcasted_iota(jnp.int32, (CHUNK, CHUNK), 1)
    neg_ref[0] = jnp.where(col_id <= row_id, 0.0, -jnp.inf)
    neg_ref[1] = jnp.where(col_id >= row_id, 0.0, -jnp.inf)

    def to_col(row):
        return jnp.sum(jnp.where(row_id == col_id, row, 0.0), axis=1, keepdims=True)

    def scores(h, d, c):
        hs = slice(h * HEAD_DIM, (h + 1) * HEAD_DIM)
        tok = _chunk_rows(c)
        rows = rows_ref[2 * h + d, c]
        m_col = to_col(rows[ROW_M:ROW_M + 1])
        g_col = jnp.exp2(rows[ROW_MIN:ROW_MIN + 1] - m_col)
        dmat = jnp.exp2(rows[ROW_A:ROW_A + 1] - m_col + neg_ref[d])
        q = q_ref[tok, hs]
        kt = kt_ref[hs, tok]
        lhs = jnp.concatenate([(_dot(q, kt) * dmat).astype(BF16),
                               (q.astype(F32) * g_col).astype(BF16)], axis=1)
        kw = (kt.astype(F32) * rows[ROW_W:ROW_W + 1]).astype(BF16)
        return lhs, kw, rows

    def readout(h, d, c, lhs, kw, rows):
        hs = slice(h * HEAD_DIM, (h + 1) * HEAD_DIM)
        v_aug = _v_aug(v_ref[_chunk_rows(c), hs])
        s_old = s_ref[2 * h + d]
        tot = _dot(lhs, jnp.concatenate([v_aug, s_old.astype(BF16)], axis=0))
        dec = rows[ROW_DECAY:ROW_DECAY + 1]
        s_ref[2 * h + d] = jnp.concatenate([dec, dec], axis=1) * s_old + _dot(kw, v_aug)
        return tot

    def normalise(tot, rows):
        e_col = to_col(rows[ROW_E:ROW_E + 1])
        return tot[:, :HEAD_DIM] / jnp.maximum(jnp.abs(tot[:, HEAD_DIM:]), e_col)

    def iteration(i, accumulate):
        scans = [(h, d, (nc - 1 - i) if d else i) for h in range(N_HEADS) for d in range(2)]
        staged = [scores(*s) for s in scans]
        tots = [readout(*s, *st) for s, st in zip(scans, staged)]
        for (h, d, c), tot, st in zip(scans, tots, staged):
            hh = normalise(tot, st[2])
            dst = (_chunk_rows(c), slice(h * HEAD_DIM, (h + 1) * HEAD_DIM))
            if accumulate:
                out_ref[dst] = (hbuf[dst] + hh).astype(out_ref.dtype)
            else:
                hbuf[dst] = hh

    def first_half(i, carry):
        iteration(i, False)
        return carry

    def second_half(i, carry):
        iteration(i, True)
        return carry

    lax.fori_loop(0, nc // 2, first_half, 0, unroll=2)
    lax.fori_loop(nc // 2, nc, second_half, 0, unroll=2)


def _mlstm(qvo, kt, gt, cv, ckt, cgt):
    nb, seq, _ = qvo.shape
    ctx_len = cv.shape[1]
    assert seq % (2 * CHUNK) == 0 and ctx_len % CHUNK == 0
    nc, ncc = seq // CHUNK, ctx_len // CHUNK
    qvo_blk = lambda j: pl.BlockSpec((None, seq, MLSTM_W), lambda b: (b, 0, j))
    whole = lambda *s: pl.BlockSpec((None,) + s, lambda b: (b,) + (0,) * len(s))
    return pl.pallas_call(
        _mlstm_kernel,
        grid=(nb,),
        in_specs=[qvo_blk(0), whole(MLSTM_W, seq), qvo_blk(1), whole(N_GATES, nc, CHUNK),
                  whole(MLSTM_W, ctx_len), whole(ctx_len, MLSTM_W), whole(N_GATES, ncc, CHUNK)],
        out_specs=whole(seq, MLSTM_W),
        out_shape=jax.ShapeDtypeStruct((nb, seq, MLSTM_W), BF16),
        scratch_shapes=[pltpu.VMEM((seq, MLSTM_W), F32),
                        pltpu.VMEM((2 * N_HEADS, HEAD_DIM, 2 * HEAD_DIM), F32),
                        pltpu.VMEM((2 * N_HEADS, nc, N_ROWS, CHUNK), F32),
                        pltpu.VMEM((2, CHUNK, CHUNK), F32)],
        compiler_params=_params(),
        name="mlstm",
    )(qvo, kt, qvo, gt.reshape(nb, N_GATES, nc, CHUNK), ckt, cv, cgt.reshape(nb, N_GATES, ncc, CHUNK))


def _mlstm_out(h, o, mh_norm):
    h = h.astype(F32)
    hn = jnp.concatenate([_rms(h[:, j * HEAD_DIM:(j + 1) * HEAD_DIM]) for j in range(N_HEADS)], axis=1)
    return (jax.nn.sigmoid(o.astype(F32)) * (hn * mh_norm)).astype(BF16)


def _out_ffn_kernel(x1_ref, conv_ref, h_ref, o_ref, mod_ref, g_ref, mhn_ref, wo_ref, wup_ref, wdn_ref, out_ref):
    rows = _sub_rows(x1_ref.shape[0])
    hms = [_mlstm_out(h_ref[r, :], o_ref[r, :], mhn_ref[...]) for r in rows]
    ys = [_dot(conv_ref[r, :], wo_ref[0:CONV_W, :]) + _dot(hm, wo_ref[CONV_W:, :]) for r, hm in zip(rows, hms)]
    x2s = [x1_ref[r, :] + mod_ref[5] * (_rms(y) * g_ref[3]) for r, y in zip(rows, ys)]
    hs = [_ffn_in(x2, g_ref[4], mod_ref[6], mod_ref[7]) for x2 in x2s]
    accs = [_swiglu(h, wup_ref, wdn_ref) for h in hs]
    for r, x2, acc in zip(rows, x2s, accs):
        out_ref[r, :] = _ffn_out(x2, acc, g_ref[5], mod_ref[8])


def _out_ffn(x1, conv, hsum, qvo, mod4, g3, mhn, wo, wup, wdn, seq, tm):
    n, d = x1.shape
    tpb = seq // tm
    tile = lambda w: pl.BlockSpec((tm, w), lambda i: (i, 0))
    return pl.pallas_call(
        _out_ffn_kernel,
        grid=(n // tm,),
        in_specs=[tile(d), tile(CONV_W), tile(MLSTM_W),
                  pl.BlockSpec((tm, MLSTM_W), lambda i: (i, 2)),
                  pl.BlockSpec((None, N_MOD, 1, d), lambda i: (i // tpb, 0, 0, 0)),
                  _resident(g3.shape), _resident(mhn.shape), _resident(wo.shape), _resident(wup.shape),
                  _resident(wdn.shape)],
        out_specs=tile(d),
        out_shape=jax.ShapeDtypeStruct((n, d), F32),
        compiler_params=_params(),
        name="outproj_ffn2",
    )(x1, conv, hsum, qvo, mod4, g3, mhn, wo, wup, wdn)


def kernel(x, c, ctx, c_ctx, w_mod, b_mod, norm_g, ffn1_up, ffn1_down, ffn2_up, ffn2_down,
           w_in, b_gates, conv_w, mh_norm, w_out):
    nb, seq, d = x.shape
    ctx_len = ctx.shape[1]
    depth = w_mod.shape[0]
    assert depth == 1, "only the single (last) layer configuration is implemented"
    assert nb + 1 <= MOD_ROWS and seq % GRID_W == 0
    tm_in, tm_out = min(TILE_ROWS_IN, seq), min(TILE_ROWS_OUT, seq)
    assert tm_in % SUB_ROWS == 0 and tm_out % SUB_ROWS == 0 and SUB_ROWS % GRID_W == 0
    conv_cols = 3 * CONV_W

    cvec = jnp.concatenate([c, c_ctx[None], jnp.zeros((MOD_ROWS - nb - 1, d), F32)], axis=0)
    mod4 = _modulation(cvec, w_mod[0], b_mod[0]).reshape(MOD_ROWS, N_MOD, 1, d)

    g3 = norm_g[0].reshape(6, 1, d)
    w1u, w1d = ffn1_up[0].astype(BF16), ffn1_down[0].astype(BF16)
    win = w_in[0]
    seg = lambda j: win[:, conv_cols + j * MLSTM_W:conv_cols + (j + 1) * MLSTM_W]
    wc = win[:, :conv_cols].astype(BF16)
    wqvo = jnp.concatenate([seg(0), seg(2), seg(3)], axis=1).astype(BF16)
    wv = seg(2).astype(BF16)
    wkt = seg(1).T.astype(BF16)
    wg = win[:, conv_cols + 4 * MLSTM_W:].T.astype(BF16)
    bg = b_gates[0].reshape(N_GATES, 1)
    cw3 = conv_w[0].reshape(3, 1, CONV_W)
    mh = mh_norm[0].reshape(1, MLSTM_W)

    (x1, conv, qvo, kt, gt), (wo, w2u, w2d) = _ffn_proj(
        x.reshape(nb * seq, d), mod4, g3, w1u, w1d, wc, wqvo, wkt, wg, bg, cw3,
        (w_out, ffn2_up, ffn2_down), seq, tm_in)
    cv, ckt, cgt = _ffn_proj_ctx(ctx.reshape(nb * ctx_len, d), mod4, g3, w1u, w1d, wv, wkt, wg, bg,
                                 ctx_len, nb)
    hsum = _mlstm(qvo.reshape(nb, seq, 3 * MLSTM_W), kt, gt, cv.reshape(nb, ctx_len, MLSTM_W), ckt, cgt)
    out = _out_ffn(x1, conv, hsum.reshape(nb * seq, MLSTM_W), qvo, mod4, g3, mh, wo, w2u, w2d, seq, tm_out)
    return out.reshape(nb, seq, d)
```

```python
import jax
import jax.numpy as jnp
from jax import lax
from jax.experimental import pallas as pl
from jax.experimental.pallas import tpu as pltpu

F32 = jnp.float32
BF16 = jnp.bfloat16

GRID_W = 64
CONV_W = 512
N_HEADS = 4
HEAD_DIM = 128
MLSTM_W = N_HEADS * HEAD_DIM
CHUNK = 128
N_MOD = 9
EPS = 1e-6
LOG2E = 1.4426950408889634
N_GATES = 4 * N_HEADS
MOD_ROWS = 24
VMEM_LIMIT = 56 * 1024 * 1024

FFN_CHUNK = 512
BF16_SUBLANES = 16
SUB_ROWS = 256
CTX_PER_STEP = 2

ROW_A, ROW_M, ROW_E, ROW_W, ROW_DECAY, ROW_MIN, N_ROWS = 0, 1, 2, 3, 4, 5, 8


def _dot(a, b):
    return jnp.dot(a, b, preferred_element_type=F32)


def _dot_nt(a, b):
    return lax.dot_general(a, b, (((1,), (1,)), ((), ())), preferred_element_type=F32)


def _rms(x):
    return x * lax.rsqrt(jnp.mean(x * x, axis=-1, keepdims=True) + EPS)


def _params():
    return pltpu.CompilerParams(dimension_semantics=("arbitrary",), vmem_limit_bytes=VMEM_LIMIT)


def _resident(shape):
    nd = len(shape)
    return pl.BlockSpec(shape, lambda *_: (0,) * nd, pipeline_mode=pl.Buffered(1))


def _mod_kernel(c_ref, w_ref, b_ref, o_ref):
    cv = c_ref[...]
    s = (cv * jax.nn.sigmoid(cv)).astype(BF16)
    o_ref[...] = _dot(s, w_ref[...].astype(BF16)) + b_ref[...]


def _modulation(cvec, w_mod, b_mod):
    d = cvec.shape[1]
    n = w_mod.shape[1]
    bn = n // N_MOD
    return pl.pallas_call(
        _mod_kernel,
        grid=(N_MOD,),
        in_specs=[pl.BlockSpec((MOD_ROWS, d), lambda j: (0, 0)),
                  pl.BlockSpec((d, bn), lambda j: (0, j)),
                  pl.BlockSpec((1, bn), lambda j: (0, j))],
        out_specs=pl.BlockSpec((MOD_ROWS, bn), lambda j: (0, j)),
        out_shape=jax.ShapeDtypeStruct((MOD_ROWS, n), F32),
        compiler_params=_params(),
        name="modulation",
    )(cvec, w_mod, b_mod.reshape(1, n))


def _ffn_in(x, g_pre, shift, scale):
    return (_rms(x) * g_pre * (1.0 + scale) + shift).astype(BF16)


def _swiglu(h, wup_ref, wdn_ref, chunk=FFN_CHUNK):
    ffn = wdn_ref.shape[0]
    chunks = [(c0, min(chunk, ffn - c0)) for c0 in range(0, ffn, chunk)]

    def up(c0, cw):
        return _dot(h, wup_ref[:, c0:c0 + cw]), _dot(h, wup_ref[:, ffn + c0:ffn + c0 + cw])

    acc = None
    ab = up(*chunks[0])
    for i, (c0, cw) in enumerate(chunks):
        a, b = ab
        if i + 1 < len(chunks):
            ab = up(*chunks[i + 1])
        act = (a * jax.nn.sigmoid(a) * b).astype(BF16)
        part = _dot(act, wdn_ref[c0:c0 + cw, :])
        acc = part if acc is None else acc + part
    return acc


def _ffn_out(x, acc, g_post, gate):
    return x + 0.5 * gate * (_rms(acc) * g_post)


def _sub_rows(n_rows):
    return [pl.ds(r, SUB_ROWS) for r in range(0, n_rows, SUB_ROWS)]


def _log_sigmoid(x):
    return jnp.minimum(x, 0.0) - jnp.log1p(jnp.exp(-jnp.abs(x)))


def _gates_t(h, wg_ref, bg_ref):
    g = _dot_nt(wg_ref[...], h) + bg_ref[...]
    half = N_GATES // 2
    return jnp.concatenate([g[:half], _log_sigmoid(g[half:])], axis=0)


def _mixer_in(x1, g_ref, mod_ref):
    return (_rms(x1) * g_ref[2] * (1.0 + mod_ref[4]) + mod_ref[3]).astype(BF16)


def _ffn_proj_kernel(x_ref, mod_ref, g_ref, wup_ref, wdn_ref, wc_ref, wqvo_ref, wkt_ref, wg_ref, bg_ref,
                     cw_ref, *refs):
    n_cast = (len(refs) - 5) // 2
    cast_in, (x1_ref, conv_ref, qvo_ref, kt_ref, gt_ref), cast_out = refs[:n_cast], refs[n_cast:n_cast + 5], \
        refs[n_cast + 5:]
    for src, dst in zip(cast_in, cast_out):
        dst[...] = src[...].astype(BF16)

    rows = _sub_rows(x_ref.shape[0])
    hs = [_ffn_in(x_ref[r, :], g_ref[0], mod_ref[0], mod_ref[1]) for r in rows]
    accs = [_swiglu(h, wup_ref, wdn_ref) for h in hs]
    h2s = []
    for r, acc in zip(rows, accs):
        x1 = _ffn_out(x_ref[r, :], acc, g_ref[1], mod_ref[2])
        x1_ref[r, :] = x1
        h2s.append(_mixer_in(x1, g_ref, mod_ref))

    for r, h in zip(rows, h2s):
        sub = h.shape[0]
        bg = _dot(h, wc_ref[:, 0:CONV_W])
        cu = _dot(h, wc_ref[:, CONV_W:2 * CONV_W]) * _dot(h, wc_ref[:, 2 * CONV_W:3 * CONV_W])
        col = lax.broadcasted_iota(jnp.int32, (sub, 1), 0) % GRID_W
        prev = jnp.where(col == 0, 0.0, pltpu.roll(cu, 1, 0))
        nxt = jnp.where(col == GRID_W - 1, 0.0, pltpu.roll(cu, sub - 1, 0))
        conv = bg * (cw_ref[0] * prev + cw_ref[1] * cu + cw_ref[2] * nxt)
        conv_ref[r, :] = conv.astype(BF16)

        for j in range(3):
            cols = slice(j * MLSTM_W, (j + 1) * MLSTM_W)
            qvo_ref[r, cols] = _dot(h, wqvo_ref[:, cols]).astype(BF16)
        kt_ref[:, r] = (_dot_nt(wkt_ref[...], h) * (HEAD_DIM ** -0.5)).astype(BF16)
        gt_ref[:, r] = _gates_t(h, wg_ref, bg_ref)


def _ffn_proj(x2d, mod4, g3, wup, wdn, wc, wqvo, wkt, wg, bg, cw3, to_cast, seq, tm):
    n, d = x2d.shape
    tpb = seq // tm
    nb = n // seq
    steps = n // tm
    tile = lambda w: pl.BlockSpec((tm, w), lambda i: (i, 0))
    tile_t = lambda r: pl.BlockSpec((None, r, tm), lambda i: (i // tpb, 0, i % tpb))
    def slab_specs(w):
        rows = next(r for r in range(BF16_SUBLANES, w.shape[1] + 1, BF16_SUBLANES)
                    if w.shape[1] % r == 0 and w.shape[1] // r <= steps)
        last = w.shape[1] // rows - 1
        return (pl.BlockSpec((None, rows, w.shape[2]), lambda i: (0, jnp.minimum(i, last), 0)),
                pl.BlockSpec((rows, w.shape[2]), lambda i: (jnp.minimum(i, last), 0)))
    cast_specs = [slab_specs(w) for w in to_cast]
    outs = pl.pallas_call(
        _ffn_proj_kernel,
        grid=(steps,),
        in_specs=[tile(d),
                  pl.BlockSpec((None, N_MOD, 1, d), lambda i: (i // tpb, 0, 0, 0)),
                  _resident(g3.shape), _resident(wup.shape), _resident(wdn.shape),
                  _resident(wc.shape), _resident(wqvo.shape), _resident(wkt.shape), _resident(wg.shape),
                  _resident(bg.shape), _resident(cw3.shape)] + [spec[0] for spec in cast_specs],
        out_specs=[tile(d), tile(CONV_W), tile(3 * MLSTM_W), tile_t(MLSTM_W), tile_t(N_GATES)]
        + [spec[1] for spec in cast_specs],
        out_shape=[jax.ShapeDtypeStruct((n, d), F32),
                   jax.ShapeDtypeStruct((n, CONV_W), BF16),
                   jax.ShapeDtypeStruct((n, 3 * MLSTM_W), BF16),
                   jax.ShapeDtypeStruct((nb, MLSTM_W, seq), BF16),
                   jax.ShapeDtypeStruct((nb, N_GATES, seq), F32)]
        + [jax.ShapeDtypeStruct(w.shape[1:], BF16) for w in to_cast],
        compiler_params=_params(),
        name="ffn1_inproj",
    )(x2d, mod4, g3, wup, wdn, wc, wqvo, wkt, wg, bg, cw3, *to_cast)
    return outs[:5], outs[5:]


def _ffn_proj_ctx_kernel(x_ref, mod_ref, g_ref, wup_ref, wdn_ref, wv_ref, wkt_ref, wg_ref, bg_ref,
                         v_ref, kt_ref, gt_ref):
    n_sub, ctx_len = kt_ref.shape[0], kt_ref.shape[2]
    rows = [pl.ds(r * ctx_len, ctx_len) for r in range(n_sub)]
    hs = [_ffn_in(x_ref[r, :], g_ref[0], mod_ref[0], mod_ref[1]) for r in rows]
    accs = [_swiglu(h, wup_ref, wdn_ref) for h in hs]
    h2s = [_mixer_in(_ffn_out(x_ref[r, :], acc, g_ref[1], mod_ref[2]), g_ref, mod_ref)
           for r, acc in zip(rows, accs)]
    for j, (r, h) in enumerate(zip(rows, h2s)):
        v_ref[r, :] = _dot(h, wv_ref[...]).astype(BF16)
        kt_ref[j] = (_dot_nt(wkt_ref[...], h) * (HEAD_DIM ** -0.5)).astype(BF16)
        gt_ref[j] = _gates_t(h, wg_ref, bg_ref)


def _ffn_proj_ctx(c2d, mod4, g3, wup, wdn, wv, wkt, wg, bg, ctx_len, ctx_row):
    n, d = c2d.shape
    assert (n // ctx_len) % CTX_PER_STEP == 0
    tm = CTX_PER_STEP * ctx_len
    tile_t = lambda r: pl.BlockSpec((CTX_PER_STEP, r, ctx_len), lambda i: (i, 0, 0))
    return pl.pallas_call(
        _ffn_proj_ctx_kernel,
        grid=(n // tm,),
        in_specs=[pl.BlockSpec((tm, d), lambda i: (i, 0)),
                  pl.BlockSpec((None, N_MOD, 1, d), lambda i: (ctx_row, 0, 0, 0)),
                  _resident(g3.shape), _resident(wup.shape), _resident(wdn.shape),
                  _resident(wv.shape), _resident(wkt.shape), _resident(wg.shape), _resident(bg.shape)],
        out_specs=[pl.BlockSpec((tm, MLSTM_W), lambda i: (i, 0)), tile_t(MLSTM_W), tile_t(N_GATES)],
        out_shape=[jax.ShapeDtypeStruct((n, MLSTM_W), BF16),
                   jax.ShapeDtypeStruct((n // ctx_len, MLSTM_W, ctx_len), BF16),
                   jax.ShapeDtypeStruct((n // ctx_len, N_GATES, ctx_len), F32)],
        compiler_params=_params(),
        name="ffn1_inproj_ctx",
    )(c2d, mod4, g3, wup, wdn, wv, wkt, wg, bg)


def _lane_scan(x, op, reverse):
    pos = lax.broadcasted_iota(jnp.int32, x.shape, 1)
    d = 1
    while d < CHUNK:
        if reverse:
            x = jnp.where(pos < CHUNK - d, op(x, pltpu.roll(x, CHUNK - d, 1)), x)
        else:
            x = jnp.where(pos >= d, op(x, pltpu.roll(x, d, 1)), x)
        d *= 2
    return x


def _lane_pick(x, lane):
    idx = lax.broadcasted_iota(jnp.int32, x.shape, 1)
    return jnp.sum(jnp.where(idx == lane, x, 0.0), axis=1, keepdims=True)


def _chunk_rows(c):
    if isinstance(c, int):
        return pl.ds(c * CHUNK, CHUNK)
    return pl.ds(pl.multiple_of(c * CHUNK, CHUNK), CHUNK)


def _gate_scan(ig, lf, m, reverse):
    n = ig.shape[0]
    b = _lane_scan(lf, jnp.add, reverse)
    a = ig - b
    b_last = _lane_pick(b, 0 if reverse else CHUNK - 1)
    wmax = b_last + jnp.max(a, axis=1, keepdims=True)
    sub = lax.broadcasted_iota(jnp.int32, (n, 1), 0)
    m_in = jnp.zeros((n, 1), F32)
    m_out = jnp.zeros((n, 1), F32)
    for c in (reversed(range(n)) if reverse else range(n)):
        m_new = jnp.maximum(b_last[c:c + 1] + m, wmax[c:c + 1])
        m_in = jnp.where(sub == c, m, m_in)
        m_out = jnp.where(sub == c, m_new, m_out)
        m = m_new
    big_m = jnp.maximum(m_in, _lane_scan(a, jnp.maximum, reverse))
    e = jnp.exp(-(b + big_m))
    w = jnp.exp(a + b_last - m_out)
    decay = jnp.exp(b_last + m_in - m_out)
    return (a, big_m, e, w, decay, m_in), m


def _v_aug(v):
    return jnp.concatenate([v, jnp.ones((CHUNK, HEAD_DIM), BF16)], axis=1)


def _mlstm_kernel(q_ref, kt_ref, v_ref, g_ref, ckt_ref, cv_ref, cg_ref, out_ref, s_ref, rows_ref, neg_ref):
    nc = kt_ref.shape[1] // CHUNK
    ncc = ckt_ref.shape[1] // CHUNK

    for h in range(N_HEADS):
        hs = slice(h * HEAD_DIM, (h + 1) * HEAD_DIM)
        for d in range(2):
            hd = 2 * h + d
            rev = d == 1
            i_row, f_row = d * N_HEADS + h, (2 + d) * N_HEADS + h
            (_, _, _, w, decay, _), m = _gate_scan(cg_ref[i_row], cg_ref[f_row], jnp.zeros((1, 1), F32), rev)
            s = jnp.zeros((HEAD_DIM, 2 * HEAD_DIM), F32)
            for c in (reversed(range(ncc)) if rev else range(ncc)):
                tok = _chunk_rows(c)
                kw = (ckt_ref[hs, tok].astype(F32) * w[c:c + 1]).astype(BF16)
                s = decay[c:c + 1] * s + _dot(kw, _v_aug(cv_ref[tok, hs]))
            s_ref[hd] = s
            (a, big_m, e, w, decay, m_in), _ = _gate_scan(g_ref[i_row], g_ref[f_row], m, rev)
            decay = jnp.broadcast_to(decay, (nc, CHUNK))
            m_in = jnp.broadcast_to(m_in, (nc, CHUNK))
            vals = ((ROW_A, a * LOG2E), (ROW_M, big_m * LOG2E), (ROW_E, e), (ROW_W, w),
                    (ROW_DECAY, decay), (ROW_MIN, m_in * LOG2E))
            for c in range(nc):
                for r, val in vals:
                    rows_ref[hd, c, r:r + 1, :] = val[c:c + 1]

    row_id = lax.broadcasted_iota(jnp.int32, (CHUNK, CHUNK), 0)
    col_id = lax.broadcasted_iota(jnp.int32, (CHUNK, CHUNK), 1)
    neg_ref[0] = jnp.where(col_id <= row_id, 0.0, -jnp.inf)
    neg_ref[1] = jnp.where(col_id >= row_id, 0.0, -jnp.inf)

    def to_col(row):
        return jnp.sum(jnp.where(row_id == col_id, row, 0.0), axis=1, keepdims=True)

    def scores(h, d, c):
        hs = slice(h * HEAD_DIM, (h + 1) * HEAD_DIM)
        tok = _chunk_rows(c)
        rows = rows_ref[2 * h + d, c]
        m_col = to_col(rows[ROW_M:ROW_M + 1])
        g_col = jnp.exp2(rows[ROW_MIN:ROW_MIN + 1] - m_col)
        dmat = jnp.exp2(rows[ROW_A:ROW_A + 1] - m_col + neg_ref[d])
        q = q_ref[tok, hs]
        kt = kt_ref[hs, tok]
        lhs = jnp.concatenate([(_dot(q, kt) * dmat).astype(BF16),
                               (q.astype(F32) * g_col).astype(BF16)], axis=1)
        kw = (kt.astype(F32) * rows[ROW_W:ROW_W + 1]).astype(BF16)
        return lhs, kw, rows

    def readout(h, d, c, lhs, kw, rows):
        hs = slice(h * HEAD_DIM, (h + 1) * HEAD_DIM)
        v_aug = _v_aug(v_ref[_chunk_rows(c), hs])
        s_old = s_ref[2 * h + d]
        tot = _dot(lhs, jnp.concatenate([v_aug, s_old.astype(BF16)], axis=0))
        dec = rows[ROW_DECAY:ROW_DECAY + 1]
        s_ref[2 * h + d] = jnp.concatenate([dec, dec], axis=1) * s_old + _dot(kw, v_aug)
        return tot

    def normalise(tot, rows):
        e_col = to_col(rows[ROW_E:ROW_E + 1])
        return tot[:, :HEAD_DIM] / jnp.maximum(jnp.abs(tot[:, HEAD_DIM:]), e_col)

    def iteration(i, accumulate):
        scans = [(h, d, (nc - 1 - i) if d else i) for h in range(N_HEADS) for d in range(2)]
        staged = [scores(*s) for s in scans]
        tots = [readout(*s, *st) for s, st in zip(scans, staged)]
        for (h, d, c), tot, st in zip(scans, tots, staged):
            hh = normalise(tot, st[2])
            dst = (_chunk_rows(c), slice(h * HEAD_DIM, (h + 1) * HEAD_DIM))
            out_ref[dst] = out_ref[dst] + hh if accumulate else hh

    def first_half(i, carry):
        iteration(i, False)
        return carry

    def second_half(i, carry):
        iteration(i, True)
        return carry

    lax.fori_loop(0, nc // 2, first_half, 0, unroll=4)
    lax.fori_loop(nc // 2, nc, second_half, 0, unroll=4)


def _mlstm(qvo, kt, gt, cv, ckt, cgt):
    nb, seq, _ = qvo.shape
    ctx_len = cv.shape[1]
    assert seq % (2 * CHUNK) == 0 and ctx_len % CHUNK == 0
    nc, ncc = seq // CHUNK, ctx_len // CHUNK
    qvo_blk = lambda j: pl.BlockSpec((None, seq, MLSTM_W), lambda b: (b, 0, j))
    whole = lambda *s: pl.BlockSpec((None,) + s, lambda b: (b,) + (0,) * len(s))
    return pl.pallas_call(
        _mlstm_kernel,
        grid=(nb,),
        in_specs=[qvo_blk(0), whole(MLSTM_W, seq), qvo_blk(1), whole(N_GATES, nc, CHUNK),
                  whole(MLSTM_W, ctx_len), whole(ctx_len, MLSTM_W), whole(N_GATES, ncc, CHUNK)],
        out_specs=whole(seq, MLSTM_W),
        out_shape=jax.ShapeDtypeStruct((nb, seq, MLSTM_W), F32),
        scratch_shapes=[pltpu.VMEM((2 * N_HEADS, HEAD_DIM, 2 * HEAD_DIM), F32),
                        pltpu.VMEM((2 * N_HEADS, nc, N_ROWS, CHUNK), F32),
                        pltpu.VMEM((2, CHUNK, CHUNK), F32)],
        compiler_params=_params(),
        name="mlstm",
    )(qvo, kt, qvo, gt.reshape(nb, N_GATES, nc, CHUNK), ckt, cv, cgt.reshape(nb, N_GATES, ncc, CHUNK))


def _mlstm_out(h, o, mh_norm):
    hn = jnp.concatenate([_rms(h[:, j * HEAD_DIM:(j + 1) * HEAD_DIM]) for j in range(N_HEADS)], axis=1)
    return (jax.nn.sigmoid(o.astype(F32)) * (hn * mh_norm)).astype(BF16)


def _out_ffn_kernel(x1_ref, conv_ref, h_ref, o_ref, mod_ref, g_ref, mhn_ref, wo_ref, wup_ref, wdn_ref, out_ref):
    rows = _sub_rows(x1_ref.shape[0])
    hms = [_mlstm_out(h_ref[r, :], o_ref[r, :], mhn_ref[...]) for r in rows]
    ys = [_dot(conv_ref[r, :], wo_ref[0:CONV_W, :]) + _dot(hm, wo_ref[CONV_W:, :]) for r, hm in zip(rows, hms)]
    x2s = [x1_ref[r, :] + mod_ref[5] * (_rms(y) * g_ref[3]) for r, y in zip(rows, ys)]
    hs = [_ffn_in(x2, g_ref[4], mod_ref[6], mod_ref[7]) for x2 in x2s]
    accs = [_swiglu(h, wup_ref, wdn_ref, FFN_CHUNK // 2) for h in hs]
    for r, x2, acc in zip(rows, x2s, accs):
        out_ref[r, :] = _ffn_out(x2, acc, g_ref[5], mod_ref[8])


def _out_ffn(x1, conv, hsum, qvo, mod4, g3, mhn, wo, wup, wdn, seq, tm):
    n, d = x1.shape
    tpb = seq // tm
    tile = lambda w: pl.BlockSpec((tm, w), lambda i: (i, 0))
    return pl.pallas_call(
        _out_ffn_kernel,
        grid=(n // tm,),
        in_specs=[tile(d), tile(CONV_W), tile(MLSTM_W),
                  pl.BlockSpec((tm, MLSTM_W), lambda i: (i, 2)),
                  pl.BlockSpec((None, N_MOD, 1, d), lambda i: (i // tpb, 0, 0, 0)),
                  _resident(g3.shape), _resident(mhn.shape), _resident(wo.shape), _resident(wup.shape),
                  _resident(wdn.shape)],
        out_specs=tile(d),
        out_shape=jax.ShapeDtypeStruct((n, d), F32),
        compiler_params=_params(),
        name="outproj_ffn2",
    )(x1, conv, hsum, qvo, mod4, g3, mhn, wo, wup, wdn)


def kernel(x, c, ctx, c_ctx, w_mod, b_mod, norm_g, ffn1_up, ffn1_down, ffn2_up, ffn2_down,
           w_in, b_gates, conv_w, mh_norm, w_out):
    nb, seq, d = x.shape
    ctx_len = ctx.shape[1]
    depth = w_mod.shape[0]
    assert depth == 1, "only the single (last) layer configuration is implemented"
    assert nb + 1 <= MOD_ROWS and seq % GRID_W == 0
    tm_in, tm_out = min(512, seq), min(1024, seq)
    assert tm_in % SUB_ROWS == 0 and tm_out % SUB_ROWS == 0 and SUB_ROWS % GRID_W == 0
    conv_cols = 3 * CONV_W

    cvec = jnp.concatenate([c, c_ctx[None], jnp.zeros((MOD_ROWS - nb - 1, d), F32)], axis=0)
    mod4 = _modulation(cvec, w_mod[0], b_mod[0]).reshape(MOD_ROWS, N_MOD, 1, d)

    g3 = norm_g[0].reshape(6, 1, d)
    w1u, w1d = ffn1_up[0].astype(BF16), ffn1_down[0].astype(BF16)
    win = w_in[0]
    seg = lambda j: win[:, conv_cols + j * MLSTM_W:conv_cols + (j + 1) * MLSTM_W]
    wc = win[:, :conv_cols].astype(BF16)
    wqvo = jnp.concatenate([seg(0), seg(2), seg(3)], axis=1).astype(BF16)
    wv = seg(2).astype(BF16)
    wkt = seg(1).T.astype(BF16)
    wg = win[:, conv_cols + 4 * MLSTM_W:].T.astype(BF16)
    bg = b_gates[0].reshape(N_GATES, 1)
    cw3 = conv_w[0].reshape(3, 1, CONV_W)
    mh = mh_norm[0].reshape(1, MLSTM_W)

    (x1, conv, qvo, kt, gt), (wo, w2u, w2d) = _ffn_proj(
        x.reshape(nb * seq, d), mod4, g3, w1u, w1d, wc, wqvo, wkt, wg, bg, cw3,
        (w_out, ffn2_up, ffn2_down), seq, tm_in)
    cv, ckt, cgt = _ffn_proj_ctx(ctx.reshape(nb * ctx_len, d), mod4, g3, w1u, w1d, wv, wkt, wg, bg,
                                 ctx_len, nb)
    hsum = _mlstm(qvo.reshape(nb, seq, 3 * MLSTM_W), kt, gt, cv.reshape(nb, ctx_len, MLSTM_W), ckt, cgt)
    out = _out_ffn(x1, conv, hsum.reshape(nb * seq, MLSTM_W), qvo, mod4, g3, mh, wo, w2u, w2d, seq, tm_out)
    return out.reshape(nb, seq, d)
```
